```python
import jax
import jax.numpy as jnp
from jax import lax
import numpy as np

D_MODEL = 1024
BATCH = 8
SEQ = 2048
DEPTH = 2
DEC_BATCH = 128
DEC_SEQ = 4
PAST_LEN = 8192
PAGE_SIZE = 128

N_A_LAYERS = DEPTH // 2
N_B_LAYERS = DEPTH - N_A_LAYERS
D_RNN = D_MODEL
LRU_HEADS = 8
LRU_BW = D_RNN // LRU_HEADS
CONV_W = 4
LRU_C = 8.0
N_HEADS = 16
QK_NOPE = 64
QK_ROPE = 32
V_HEAD = 64
KV_LORA = 256
Q_LORA = 512
ROPE_THETA = 10000.0
SOFTMAX_SCALE = (QK_NOPE + QK_ROPE) ** -0.5
Q_BLOCK = 128
PEER_HEADS = 8
PEER_KEYS = 128
PEER_EXPERTS = PEER_KEYS * PEER_KEYS
PEER_DK = 256
PEER_TOPK = 16
PEER_BLOCK = 256

RMS_EPS = 1e-6
NEG_INF = -1e30

kernel_name = 'yoco_rglru_mla_peer_step'


def rmsnorm(x, g):
    xf = x.astype(jnp.float32)
    y = xf * lax.rsqrt(jnp.mean(xf * xf, axis=-1, keepdims=True) + RMS_EPS)
    return (y * g.astype(jnp.float32)).astype(x.dtype)


def rope(x, pos):
    half = x.shape[-1] // 2
    inv = ROPE_THETA ** (-jnp.arange(half, dtype=jnp.float32) / half)
    ang = pos.astype(jnp.float32)[:, None] * inv[None, :]
    shape = (1, pos.shape[0]) + (1,) * (x.ndim - 3) + (half,)
    cos = jnp.cos(ang).reshape(shape)
    sin = jnp.sin(ang).reshape(shape)
    xf = x.astype(jnp.float32)
    x1, x2 = xf[..., :half], xf[..., half:]
    return jnp.concatenate([x1 * cos - x2 * sin, x2 * cos + x1 * sin], axis=-1).astype(x.dtype)


def lru_scan(a, b, h0):
    b = b.at[:, 0].add(a[:, 0] * h0)

    def combine(left, right):
        a_l, b_l = left
        a_r, b_r = right
        return a_l * a_r, a_r * b_l + b_r

    _, h = lax.associative_scan(combine, (a, b), axis=1)
    return h


def rglru_mixer(x_n, conv_buf, h0, w_in, conv_w, conv_b, gate_a_w, gate_a_b, gate_x_w, gate_x_b, lam, w_out):
    b, t, _ = x_n.shape
    proj = x_n @ w_in
    gate = jax.nn.gelu(proj[..., :D_RNN])
    xb = proj[..., D_RNN:]
    xcat = jnp.concatenate([conv_buf.astype(xb.dtype), xb], axis=1)
    xc = conv_b + sum(xcat[:, k:k + t] * conv_w[k] for k in range(CONV_W))
    new_buf = xcat[:, -(CONV_W - 1):]
    xh = xc.reshape(b, t, LRU_HEADS, LRU_BW)
    r = jax.nn.sigmoid(jnp.einsum('bthi,hij->bthj', xh, gate_a_w, preferred_element_type=jnp.float32).reshape(b, t, D_RNN) + gate_a_b.astype(jnp.float32))
    i = jax.nn.sigmoid(jnp.einsum('bthi,hij->bthj', xh, gate_x_w, preferred_element_type=jnp.float32).reshape(b, t, D_RNN) + gate_x_b.astype(jnp.float32))
    log_a = -LRU_C * r * jax.nn.softplus(-lam.astype(jnp.float32))
    a = jnp.exp(log_a)
    bx = jnp.sqrt(-jnp.expm1(2.0 * log_a)) * i * xc.astype(jnp.float32)
    h = lru_scan(a, bx, h0.astype(jnp.float32))
    y = (h.astype(x_n.dtype) * gate) @ w_out
    return y, new_buf, h[:, -1].astype(x_n.dtype)


def peer_ffn(x_n, w_q, subkeys, u_tab, v_tab):
    shp = x_n.shape
    xf = x_n.reshape(-1, shp[-1])
    n = xf.shape[0]
    nb = -(-n // PEER_BLOCK)
    xb = jnp.pad(xf, ((0, nb * PEER_BLOCK - n), (0, 0))).reshape(nb, PEER_BLOCK, shp[-1])

    def per_block(xt):
        m = xt.shape[0]
        q = (xt @ w_q).reshape(m, PEER_HEADS, 2, PEER_DK // 2)
        s = jnp.einsum('nhpd,hpkd->nhpk', q, subkeys, preferred_element_type=jnp.float32)
        sv, si = lax.top_k(s, PEER_TOPK)
        cand = (sv[:, :, 0, :, None] + sv[:, :, 1, None, :]).reshape(m, PEER_HEADS, PEER_TOPK * PEER_TOPK)
        cidx = (si[:, :, 0, :, None] * PEER_KEYS + si[:, :, 1, None, :]).reshape(m, PEER_HEADS, PEER_TOPK * PEER_TOPK)
        cs, cp = lax.top_k(cand, PEER_TOPK)
        e = jnp.take_along_axis(cidx, cp, axis=-1)
        g = jax.nn.softmax(cs, axis=-1)
        act = jax.nn.gelu(jnp.einsum('nd,nhkd->nhk', xt, u_tab[e], preferred_element_type=jnp.float32))
        return jnp.einsum('nhk,nhkd->nd', (g * act).astype(xt.dtype), v_tab[e])

    out = lax.map(per_block, xb).reshape(nb * PEER_BLOCK, shp[-1])[:n]
    return out.reshape(shp)


def shared_kv(h, pos, kv_norm, w_dkv, ckv_norm, w_kr):
    hn = rmsnorm(h, kv_norm)
    c = rmsnorm(hn @ w_dkv, ckv_norm)
    kr = rope(hn @ w_kr, pos)
    return c, kr


def mla_queries(x_n, pos, w_dq, q_norm, w_uq, w_uk):
    b, t, _ = x_n.shape
    q = (rmsnorm(x_n @ w_dq, q_norm) @ w_uq).reshape(b, t, N_HEADS, QK_NOPE + QK_ROPE)
    q_lat = jnp.einsum('bthn,chn->bthc', q[..., :QK_NOPE], w_uk)
    q_rope = rope(q[..., QK_NOPE:], pos)
    return q_lat, q_rope


def mla_attend(q_lat, q_rope, segments, w_uv):
    scores = []
    for c, kr, mask in segments:
        s = (jnp.einsum('bqhc,bkc->bhqk', q_lat, c, preferred_element_type=jnp.float32)
             + jnp.einsum('bqhr,bkr->bhqk', q_rope, kr, preferred_element_type=jnp.float32)) * SOFTMAX_SCALE
        if mask is not None:
            s = jnp.where(mask, s, NEG_INF)
        scores.append(s)
    p = jax.nn.softmax(jnp.concatenate(scores, axis=-1), axis=-1)
    o_lat, off = [], 0
    for c, _, _ in segments:
        tk = c.shape[1]
        o_lat.append(jnp.einsum('bhqk,bkc->bqhc', p[..., off:off + tk].astype(c.dtype), c))
        off += tk
    return jnp.einsum('bqhc,chv->bqhv', sum(o_lat), w_uv)


def mla_prompt_attention(q_lat, q_rope, pos, c, kr, w_uv):
    b, t = q_lat.shape[:2]
    nb = t // Q_BLOCK

    def per_block(args):
        ql, qr, qp = args
        mask = pos[None, :] <= qp[:, None]
        return mla_attend(ql, qr, [(c, kr, mask)], w_uv)

    def to_blocks(a):
        return jnp.swapaxes(a.reshape((b, nb, Q_BLOCK) + a.shape[2:]), 0, 1)

    o = lax.map(per_block, (to_blocks(q_lat), to_blocks(q_rope), pos.reshape(nb, Q_BLOCK)))
    return jnp.swapaxes(o, 0, 1).reshape(b, t, N_HEADS, V_HEAD)


def setup_inputs(seed: int = 0) -> dict:
    key = jax.random.key(seed)
    ks = iter(jax.random.split(key, 48))
    f32 = jnp.float32

    def nrm(shape, scale):
        return jax.random.normal(next(ks), shape, f32) * scale

    def gain(shape):
        return 1.0 + 0.02 * jax.random.normal(next(ks), shape, f32)

    n_pages = PAST_LEN // PAGE_SIZE
    n_used = DEC_BATCH * n_pages
    n_pool = n_used + n_used // 4
    x_prompt = nrm((BATCH, SEQ, D_MODEL), 1.0)
    x_sample = nrm((DEC_BATCH, DEC_SEQ, D_MODEL), 1.0)
    cache_ckv = nrm((n_pool, PAGE_SIZE, KV_LORA), 1.0)
    cache_krope = nrm((n_pool, PAGE_SIZE, QK_ROPE), 1.0)
    page_table = jax.random.permutation(next(ks), n_pool)[:n_used].reshape(DEC_BATCH, n_pages).astype(jnp.int32)
    state_h = nrm((DEC_BATCH, N_A_LAYERS, D_RNN), 0.5)
    state_conv = nrm((DEC_BATCH, N_A_LAYERS, CONV_W - 1, D_RNN), 1.0)
    ln_mix = gain((DEPTH, D_MODEL))
    ln_ffn = gain((DEPTH, D_MODEL))
    ln_final = gain((D_MODEL,))
    lru_w_in = nrm((N_A_LAYERS, D_MODEL, 2 * D_RNN), D_MODEL ** -0.5)
    lru_conv_w = nrm((N_A_LAYERS, CONV_W, D_RNN), CONV_W ** -0.5)
    lru_conv_b = nrm((N_A_LAYERS, D_RNN), 0.02)
    lru_gate_a_w = nrm((N_A_LAYERS, LRU_HEADS, LRU_BW, LRU_BW), LRU_BW ** -0.5)
    lru_gate_a_b = nrm((N_A_LAYERS, D_RNN), 0.02)
    lru_gate_x_w = nrm((N_A_LAYERS, LRU_HEADS, LRU_BW, LRU_BW), LRU_BW ** -0.5)
    lru_gate_x_b = nrm((N_A_LAYERS, D_RNN), 0.02)
    a_c = jax.random.uniform(next(ks), (N_A_LAYERS, D_RNN), f32, 0.9, 0.999)
    a0 = a_c ** (1.0 / LRU_C)
    lru_lambda = jnp.log(a0) - jnp.log1p(-a0)
    lru_w_out = nrm((N_A_LAYERS, D_RNN, D_MODEL), D_RNN ** -0.5)
    kv_norm = gain((D_MODEL,))
    w_dkv = nrm((D_MODEL, KV_LORA), D_MODEL ** -0.5)
    ckv_norm = gain((KV_LORA,))
    w_kr = nrm((D_MODEL, QK_ROPE), D_MODEL ** -0.5)
    w_uk = nrm((KV_LORA, N_HEADS, QK_NOPE), KV_LORA ** -0.5)
    w_uv = nrm((KV_LORA, N_HEADS, V_HEAD), KV_LORA ** -0.5)
    mla_w_dq = nrm((N_B_LAYERS, D_MODEL, Q_LORA), D_MODEL ** -0.5)
    mla_q_norm = gain((N_B_LAYERS, Q_LORA))
    mla_w_uq = nrm((N_B_LAYERS, Q_LORA, N_HEADS * (QK_NOPE + QK_ROPE)), Q_LORA ** -0.5)
    mla_w_o = nrm((N_B_LAYERS, N_HEADS * V_HEAD, D_MODEL), (N_HEADS * V_HEAD) ** -0.5)
    peer_w_q = nrm((DEPTH, D_MODEL, PEER_HEADS * PEER_DK), D_MODEL ** -0.5)
    peer_subkeys = nrm((DEPTH, PEER_HEADS, 2, PEER_KEYS, PEER_DK // 2), (PEER_DK // 2) ** -0.5)
    peer_u = nrm((DEPTH, PEER_EXPERTS, D_MODEL), D_MODEL ** -0.5)
    peer_v = nrm((DEPTH, PEER_EXPERTS, D_MODEL), 0.5 * PEER_HEADS ** -0.5)
    return {'x_prompt': x_prompt, 'x_sample': x_sample, 'cache_ckv': cache_ckv, 'cache_krope': cache_krope,
            'page_table': page_table, 'state_h': state_h, 'state_conv': state_conv,
            'ln_mix': ln_mix, 'ln_ffn': ln_ffn, 'ln_final': ln_final,
            'lru_w_in': lru_w_in, 'lru_conv_w': lru_conv_w, 'lru_conv_b': lru_conv_b,
            'lru_gate_a_w': lru_gate_a_w, 'lru_gate_a_b': lru_gate_a_b,
            'lru_gate_x_w': lru_gate_x_w, 'lru_gate_x_b': lru_gate_x_b,
            'lru_lambda': lru_lambda, 'lru_w_out': lru_w_out,
            'kv_norm': kv_norm, 'w_dkv': w_dkv, 'ckv_norm': ckv_norm, 'w_kr': w_kr, 'w_uk': w_uk, 'w_uv': w_uv,
            'mla_w_dq': mla_w_dq, 'mla_q_norm': mla_q_norm, 'mla_w_uq': mla_w_uq, 'mla_w_o': mla_w_o,
            'peer_w_q': peer_w_q, 'peer_subkeys': peer_subkeys, 'peer_u': peer_u, 'peer_v': peer_v}


def reference(x_prompt, x_sample, cache_ckv, cache_krope, page_table, state_h, state_conv,
              ln_mix, ln_ffn, ln_final,
              lru_w_in, lru_conv_w, lru_conv_b, lru_gate_a_w, lru_gate_a_b, lru_gate_x_w, lru_gate_x_b,
              lru_lambda, lru_w_out,
              kv_norm, w_dkv, ckv_norm, w_kr, w_uk, w_uv,
              mla_w_dq, mla_q_norm, mla_w_uq, mla_w_o,
              peer_w_q, peer_subkeys, peer_u, peer_v):
    bp, sp, _ = x_prompt.shape
    bd, sd, _ = x_sample.shape
    pos_p = jnp.arange(sp, dtype=jnp.int32)
    pos_d = PAST_LEN + jnp.arange(sd, dtype=jnp.int32)
    past_len = page_table.shape[1] * cache_ckv.shape[1]
    c_past = cache_ckv[page_table].reshape(bd, past_len, KV_LORA)
    kr_past = cache_krope[page_table].reshape(bd, past_len, QK_ROPE)
    mask_new = pos_d[None, :] <= pos_d[:, None]

    conv_p0 = jnp.zeros((bp, CONV_W - 1, D_RNN), x_prompt.dtype)
    h_p0 = jnp.zeros((bp, D_RNN), x_prompt.dtype)
    hl_p, cb_p, hl_d, cb_d = [], [], [], []
    hp, hd = x_prompt, x_sample
    c_p = kr_p = c_d = kr_d = None
    for l in range(DEPTH):
        if l < N_A_LAYERS:
            la = l
            wts = (lru_w_in[la], lru_conv_w[la], lru_conv_b[la], lru_gate_a_w[la], lru_gate_a_b[la],
                   lru_gate_x_w[la], lru_gate_x_b[la], lru_lambda[la], lru_w_out[la])
            yp, bufp, hlp = rglru_mixer(rmsnorm(hp, ln_mix[l]), conv_p0, h_p0, *wts)
            yd, bufd, hld = rglru_mixer(rmsnorm(hd, ln_mix[l]), state_conv[:, la], state_h[:, la], *wts)
            hl_p.append(hlp)
            cb_p.append(bufp)
            hl_d.append(hld)
            cb_d.append(bufd)
        else:
            lb = l - N_A_LAYERS
            qw = (mla_w_dq[lb], mla_q_norm[lb], mla_w_uq[lb], w_uk)
            ql_p, qr_p = mla_queries(rmsnorm(hp, ln_mix[l]), pos_p, *qw)
            op = mla_prompt_attention(ql_p, qr_p, pos_p, c_p, kr_p, w_uv)
            ql_d, qr_d = mla_queries(rmsnorm(hd, ln_mix[l]), pos_d, *qw)
            od = mla_attend(ql_d, qr_d, [(c_past, kr_past, None), (c_d, kr_d, mask_new)], w_uv)
            yp = op.reshape(bp, sp, N_HEADS * V_HEAD) @ mla_w_o[lb]
            yd = od.reshape(bd, sd, N_HEADS * V_HEAD) @ mla_w_o[lb]
        hp = hp + yp
        hd = hd + yd
        pw = (peer_w_q[l], peer_subkeys[l], peer_u[l], peer_v[l])
        hp = hp + peer_ffn(rmsnorm(hp, ln_ffn[l]), *pw)
        hd = hd + peer_ffn(rmsnorm(hd, ln_ffn[l]), *pw)
        if l == N_A_LAYERS - 1:
            c_p, kr_p = shared_kv(hp, pos_p, kv_norm, w_dkv, ckv_norm, w_kr)
            c_d, kr_d = shared_kv(hd, pos_d, kv_norm, w_dkv, ckv_norm, w_kr)

    y_prompt = rmsnorm(hp, ln_final)
    y_sample = rmsnorm(hd, ln_final)
    state_h_prompt = jnp.stack(hl_p, axis=1)
    state_conv_prompt = jnp.stack(cb_p, axis=1)
    state_h_sample = jnp.stack(hl_d, axis=1)
    state_conv_sample = jnp.stack(cb_d, axis=1)
    return (y_prompt, y_sample, state_h_prompt, state_conv_prompt, c_p, kr_p,
            state_h_sample, state_conv_sample, c_d, kr_d)
```

```python
import functools

import jax
import jax.numpy as jnp
from jax import lax
from jax.experimental import pallas as pl
from jax.experimental.pallas import tpu as pltpu

F32 = jnp.float32
BF16 = jnp.bfloat16

RMS_EPS = 1e-6
NEG_INF = -1e30
LRU_C = 8.0
CONV_W = 4
ROPE_THETA = 10000.0
PEER_TOPK = 16

LANES = 128
SUBLANES = 8
VMEM_LIMIT = 56 * 1024 * 1024

_NT = (((1,), (1,)), ((), ()))


def _cparams(*sem):
    return pltpu.CompilerParams(dimension_semantics=sem, vmem_limit_bytes=VMEM_LIMIT)


def _rms(x, g):
    return x * lax.rsqrt(jnp.mean(x * x, axis=-1, keepdims=True) + RMS_EPS) * g


def _gelu(x):
    return 0.5 * x * (1.0 + jnp.tanh(0.7978845608028654 * (x + 0.044715 * (x * x * x))))


def _sigmoid(x):
    return 1.0 / (1.0 + jnp.exp(-x))


def _softplus(x):
    return jnp.maximum(x, 0.0) + jnp.log(1.0 + jnp.exp(-jnp.abs(x)))


def _full(shape):
    n = len(shape)
    return pl.BlockSpec(shape, lambda *_: (0,) * n)


def _lru_gates(xc, gaw_ref, gab_ref, gxw_ref, gxb_ref, lam_ref):
    nh, bw, _ = gaw_ref.shape
    xcb = xc.astype(BF16)
    r_parts, i_parts = [], []
    for h in range(nh):
        blk = xcb[:, h * bw:(h + 1) * bw]
        r_parts.append(jnp.dot(blk, gaw_ref[h], preferred_element_type=F32))
        i_parts.append(jnp.dot(blk, gxw_ref[h], preferred_element_type=F32))
    r = _sigmoid(jnp.concatenate(r_parts, axis=-1) + gab_ref[...])
    i = _sigmoid(jnp.concatenate(i_parts, axis=-1) + gxb_ref[...])
    log_a = (-LRU_C) * r * _softplus(-lam_ref[...])
    a = jnp.exp(log_a)
    bx = jnp.sqrt(1.0 - jnp.exp(2.0 * log_a)) * i * xc
    return a, bx


def _lru_prompt_kernel(x_ref, g_ref, win_ref, cw_ref, cb_ref, gaw_ref, gab_ref, gxw_ref, gxb_ref,
                       lam_ref, wout_ref, y_ref, hl_ref, cbuf_ref, xcat, a_s, b_s, h_s, hcar):
    t = pl.program_id(1)
    tt, d = x_ref.shape[1], x_ref.shape[2]

    @pl.when(t == 0)
    def _():
        xcat[0:SUBLANES, :] = jnp.zeros((SUBLANES, d), F32)
        hcar[...] = jnp.zeros((1, d), F32)

    x = x_ref[0]
    xn = _rms(x, g_ref[...])
    proj = jnp.dot(xn.astype(BF16), win_ref[...], preferred_element_type=F32)
    gate = _gelu(proj[:, :d])
    xb = proj[:, d:]
    xcat[SUBLANES:SUBLANES + tt, :] = xb
    xc = (cb_ref[...]
          + cw_ref[0:1, :] * xcat[SUBLANES - 3:SUBLANES - 3 + tt, :]
          + cw_ref[1:2, :] * xcat[SUBLANES - 2:SUBLANES - 2 + tt, :]
          + cw_ref[2:3, :] * xcat[SUBLANES - 1:SUBLANES - 1 + tt, :]
          + cw_ref[3:4, :] * xb)
    tail = xcat[tt:tt + SUBLANES, :]
    xcat[0:SUBLANES, :] = tail
    cbuf_ref[0] = tail

    a, bx = _lru_gates(xc, gaw_ref, gab_ref, gxw_ref, gxb_ref, lam_ref)
    a_s[...] = a
    b_s[...] = bx

    def body(i, h):
        h = a_s[pl.ds(i, 1), :] * h + b_s[pl.ds(i, 1), :]
        h_s[pl.ds(i, 1), :] = h
        return h

    h = lax.fori_loop(0, tt, body, hcar[...], unroll=8)
    hcar[...] = h
    hl_ref[0] = h
    y = jnp.dot((h_s[...] * gate).astype(BF16), wout_ref[...], preferred_element_type=F32)
    y_ref[0] = x + y


def _lru_prompt(x, g, win, cw, cb, gaw, gab, gxw, gxb, lam, wout, *, tt=256):
    b, t, d = x.shape
    nh, bw, _ = gaw.shape
    row = lambda v: v.reshape(1, -1)
    wspecs = [_full((1, d)), _full((d, 2 * d)), _full((CONV_W, d)), _full((1, d)),
              _full((nh, bw, bw)), _full((1, d)), _full((nh, bw, bw)), _full((1, d)),
              _full((1, d)), _full((d, d))]
    y, hl, cbuf = pl.pallas_call(
        _lru_prompt_kernel,
        grid=(b, t // tt),
        in_specs=[pl.BlockSpec((1, tt, d), lambda i, j: (i, j, 0))] + wspecs,
        out_specs=[pl.BlockSpec((1, tt, d), lambda i, j: (i, j, 0)),
                   pl.BlockSpec((1, 1, d), lambda i, j: (i, 0, 0)),
                   pl.BlockSpec((1, SUBLANES, d), lambda i, j: (i, 0, 0))],
        out_shape=[jax.ShapeDtypeStruct((b, t, d), F32),
                   jax.ShapeDtypeStruct((b, 1, d), F32),
                   jax.ShapeDtypeStruct((b, SUBLANES, d), F32)],
        scratch_shapes=[pltpu.VMEM((tt + SUBLANES, d), F32), pltpu.VMEM((tt, d), F32),
                        pltpu.VMEM((tt, d), F32), pltpu.VMEM((tt, d), F32), pltpu.VMEM((1, d), F32)],
        compiler_params=_cparams("arbitrary", "arbitrary"),
        name="lru_prompt",
    )(x, row(g), win, cw, row(cb), gaw, row(gab), gxw, row(gxb), row(lam), wout)
    return y, hl, cbuf[:, SUBLANES - (CONV_W - 1):, :]


def _lru_sample_kernel(x_ref, conv_ref, h0_ref, g_ref, win_ref, cw_ref, cb_ref, gaw_ref, gab_ref,
                       gxw_ref, gxb_ref, lam_ref, wout_ref, y_ref, hl_ref, cbuf_ref):
    t, b, d = x_ref.shape
    x = x_ref[...].reshape(t * b, d)
    xn = _rms(x, g_ref[...])
    proj = jnp.dot(xn.astype(BF16), win_ref[...], preferred_element_type=F32)
    gate = _gelu(proj[:, :d])
    xb = proj[:, d:]
    seq = [conv_ref[k] for k in range(CONV_W - 1)] + [xb[k * b:(k + 1) * b, :] for k in range(t)]
    xc_parts = []
    for k in range(t):
        acc = cb_ref[...] + cw_ref[0:1, :] * seq[k]
        for j in range(1, CONV_W):
            acc = acc + cw_ref[j:j + 1, :] * seq[k + j]
        xc_parts.append(acc)
    for k in range(CONV_W - 1):
        cbuf_ref[k] = seq[t + k]
    xc = jnp.concatenate(xc_parts, axis=0)
    a, bx = _lru_gates(xc, gaw_ref, gab_ref, gxw_ref, gxb_ref, lam_ref)
    h = h0_ref[...]
    hs = []
    for k in range(t):
        h = a[k * b:(k + 1) * b, :] * h + bx[k * b:(k + 1) * b, :]
        hs.append(h)
    hl_ref[...] = h
    hall = jnp.concatenate(hs, axis=0)
    y = jnp.dot((hall * gate).astype(BF16), wout_ref[...], preferred_element_type=F32)
    y_ref[...] = (x + y).reshape(t, b, d)


def _lru_sample(x, conv, h0, g, win, cw, cb, gaw, gab, gxw, gxb, lam, wout):
    b, t, d = x.shape
    assert t >= CONV_W - 1
    row = lambda v: v.reshape(1, -1)
    y, hl, cbuf = pl.pallas_call(
        _lru_sample_kernel,
        out_shape=[jax.ShapeDtypeStruct((t, b, d), F32),
                   jax.ShapeDtypeStruct((b, d), F32),
                   jax.ShapeDtypeStruct((CONV_W - 1, b, d), F32)],
        compiler_params=pltpu.CompilerParams(vmem_limit_bytes=VMEM_LIMIT),
        name="lru_sample",
    )(jnp.swapaxes(x, 0, 1), jnp.swapaxes(conv, 0, 1), h0, row(g), win, cw, row(cb),
      gaw, row(gab), gxw, row(gxb), row(lam), wout)
    return jnp.swapaxes(y, 0, 1), hl, jnp.swapaxes(cbuf, 0, 1)


def _cand_pairs(k):
    return [(a, min(k, (k + 1) // (a + 1))) for a in range(k) if (k + 1) // (a + 1) >= 1]


def _peer_route_kernel(h_ref, g_ref, wq_ref, sk_ref, xnt_ref, s1c_ref, eb_ref, t0_ref, aw_ref,
                       q_s, st_s, sv_s, cand_s):
    tn, d = h_ref.shape
    nh, _, nk, dk2 = sk_ref.shape
    xn = _rms(h_ref[...], g_ref[...])
    xnt_ref[...] = xn.T.astype(BF16)
    q = jnp.dot(xn.astype(BF16), wq_ref[...], preferred_element_type=F32).astype(BF16)
    for hp in range(2 * nh):
        q_s[hp] = q[:, hp * dk2:(hp + 1) * dk2]

    def scores(hp, carry):
        s = lax.dot_general(sk_ref[hp // 2, hp % 2], q_s[hp], _NT, preferred_element_type=F32)
        st_s[hp] = s
        for ts in range(tn // LANES):
            cur = s[:, ts * LANES:(ts + 1) * LANES]
            for r in range(PEER_TOPK):
                m = jnp.max(cur, axis=0, keepdims=True)
                sv_s[hp, r:r + 1, ts * LANES:(ts + 1) * LANES] = m
                cur = jnp.where(cur == m, -jnp.inf, cur)
        return carry

    lax.fori_loop(0, 2 * nh, scores, 0)

    pairs = _cand_pairs(PEER_TOPK)
    ncand = sum(nb for _, nb in pairs)
    cand_s[...] = jnp.full(cand_s.shape, -jnp.inf, F32)

    def thresholds(h, carry):
        for ts in range(tn // LANES):
            sl = slice(ts * LANES, (ts + 1) * LANES)
            sv0 = sv_s[2 * h, :, sl]
            sv1 = sv_s[2 * h + 1, :, sl]
            off = 0
            for a, nb in pairs:
                cand_s[off:off + nb, :] = sv0[a:a + 1, :] + sv1[0:nb, :]
                off += nb
            cand0 = cand_s[...]
            cur = cand0
            t_hi = None
            for r in range(PEER_TOPK + 1):
                m = jnp.max(cur, axis=0, keepdims=True)
                if r == PEER_TOPK - 1:
                    t_hi = m
                cur = jnp.where(cur == m, -jnp.inf, cur)
            t_mid = 0.5 * (t_hi + m)
            m0 = sv0[0:1, :]
            m1 = sv1[0:1, :]
            z = jnp.sum(jnp.where(cand0 >= t_mid, jnp.exp(cand0 - (m0 + m1)), 0.0), axis=0, keepdims=True)
            s0 = st_s[2 * h, :, sl]
            s1c = st_s[2 * h + 1, :, sl] - m1
            s1c_ref[h, :, sl] = s1c
            eb_ref[h, :, sl] = jnp.exp(s1c)
            t0_ref[h, :, sl] = (t_mid - m1) - s0
            aw_ref[h, :, sl] = jnp.exp(s0 - m0) * (1.0 / z)
        return carry

    assert ncand <= cand_s.shape[0]
    lax.fori_loop(0, nh, thresholds, 0)


def _peer_dense_kernel(h_ref, xnt_ref, u_ref, vt_ref, s1c_ref, eb_ref, t0_ref, aw_ref, gf_ref, o_ref,
                       act_s, p_s, acc_s, *, final_norm):
    c = pl.program_id(1)
    tn = h_ref.shape[0]
    nh, ic, _ = t0_ref.shape
    nk = s1c_ref.shape[1]

    @pl.when(c == 0)
    def _():
        acc_s[...] = jnp.zeros(acc_s.shape, F32)

    act_s[...] = jnp.dot(u_ref[...], xnt_ref[...], preferred_element_type=F32)

    for ig in range(ic // SUBLANES):
        i8 = slice(ig * SUBLANES, (ig + 1) * SUBLANES)
        for ts in range(tn // LANES):
            sl = slice(ts * LANES, (ts + 1) * LANES)
            t0s = [t0_ref[h, i8, sl] for h in range(nh)]
            aws = [aw_ref[h, i8, sl] for h in range(nh)]
            for r in range(SUBLANES):
                g = jnp.zeros((nk, LANES), F32)
                for h in range(nh):
                    sel = jnp.where(s1c_ref[h, :, sl] >= t0s[h][r:r + 1, :], eb_ref[h, :, sl], 0.0)
                    g = g + sel * aws[h][r:r + 1, :]
                rows = slice((ig * SUBLANES + r) * nk, (ig * SUBLANES + r + 1) * nk)
                p_s[rows, sl] = (_gelu(act_s[rows, sl]) * g).astype(BF16)
    acc_s[...] += jnp.dot(vt_ref[...], p_s[...], preferred_element_type=F32)

    @pl.when(c == pl.num_programs(1) - 1)
    def _():
        out = h_ref[...] + acc_s[...].T
        if final_norm:
            out = _rms(out, gf_ref[...])
        o_ref[...] = out


def _peer(h, g, wq, sk, u, vt, gf, *, final_norm, tn=512, ec=1024):
    n, d = h.shape
    nh, _, nk, dk2 = sk.shape
    ne = u.shape[0]
    assert ne == nk * nk and n % tn == 0 and ne % ec == 0 and ec % (nk * SUBLANES) == 0
    ic = ec // nk
    ncand_pad = 56
    tok3 = lambda i: (0, 0, i)
    xnt, s1c, eb, t0, aw = pl.pallas_call(
        _peer_route_kernel,
        grid=(n // tn,),
        in_specs=[pl.BlockSpec((tn, d), lambda i: (i, 0)), _full((1, d)), _full(wq.shape), _full(sk.shape)],
        out_specs=[pl.BlockSpec((d, tn), lambda i: (0, i))] + [pl.BlockSpec((nh, nk, tn), tok3)] * 4,
        out_shape=[jax.ShapeDtypeStruct((d, n), BF16)] + [jax.ShapeDtypeStruct((nh, nk, n), F32)] * 4,
        scratch_shapes=[pltpu.VMEM((2 * nh, tn, dk2), BF16), pltpu.VMEM((2 * nh, nk, tn), F32),
                        pltpu.VMEM((2 * nh, PEER_TOPK, tn), F32), pltpu.VMEM((ncand_pad, LANES), F32)],
        compiler_params=_cparams("arbitrary"),
        name="peer_route",
    )(h, g.reshape(1, d), wq, sk)
    return pl.pallas_call(
        functools.partial(_peer_dense_kernel, final_norm=final_norm),
        grid=(n // tn, ne // ec),
        in_specs=[pl.BlockSpec((tn, d), lambda i, c: (i, 0)),
                  pl.BlockSpec((d, tn), lambda i, c: (0, i)),
                  pl.BlockSpec((ec, d), lambda i, c: (c, 0)),
                  pl.BlockSpec((d, ec), lambda i, c: (0, c)),
                  pl.BlockSpec((nh, nk, tn), lambda i, c: (0, 0, i)),
                  pl.BlockSpec((nh, nk, tn), lambda i, c: (0, 0, i)),
                  pl.BlockSpec((nh, ic, tn), lambda i, c: (0, c, i)),
                  pl.BlockSpec((nh, ic, tn), lambda i, c: (0, c, i)),
                  _full((1, d))],
        out_specs=pl.BlockSpec((tn, d), lambda i, c: (i, 0)),
        out_shape=jax.ShapeDtypeStruct((n, d), F32),
        scratch_shapes=[pltpu.VMEM((ec, tn), F32), pltpu.VMEM((ec, tn), BF16), pltpu.VMEM((d, tn), F32)],
        compiler_params=_cparams("arbitrary", "arbitrary"),
        name="peer_dense",
    )(h, xnt, u, vt, s1c, eb, t0, aw, gf.reshape(1, d))


def _shared_kv_kernel(h_ref, g_ref, wdkv_ref, gc_ref, wkr_ref, wkrs_ref, cos_ref, sin_ref,
                      c_ref, kr_ref, cb_ref, krb_ref):
    hn = _rms(h_ref[...], g_ref[...]).astype(BF16)
    c = _rms(jnp.dot(hn, wdkv_ref[...], preferred_element_type=F32), gc_ref[...])
    kr = (jnp.dot(hn, wkr_ref[...], preferred_element_type=F32) * cos_ref[...]
          + jnp.dot(hn, wkrs_ref[...], preferred_element_type=F32) * sin_ref[...])
    c_ref[...] = c
    kr_ref[...] = kr
    cb_ref[...] = c.astype(BF16)
    krb_ref[...] = kr.astype(BF16)


def _shared_kv(h, g, wdkv, gc, wkr, wkrs, cos, sin, *, tn):
    n, d = h.shape
    kvl, rr = wdkv.shape[1], wkr.shape[1]
    period = cos.shape[0] // tn
    tok = lambda i: (i, 0)
    return pl.pallas_call(
        _shared_kv_kernel,
        grid=(n // tn,),
        in_specs=[pl.BlockSpec((tn, d), tok), _full((1, d)), _full(wdkv.shape), _full((1, kvl)),
                  _full(wkr.shape), _full(wkrs.shape),
                  pl.BlockSpec((tn, rr), lambda i: (i % period, 0)),
                  pl.BlockSpec((tn, rr), lambda i: (i % period, 0))],
        out_specs=[pl.BlockSpec((tn, kvl), tok), pl.BlockSpec((tn, rr), tok),
                   pl.BlockSpec((tn, kvl), tok), pl.BlockSpec((tn, rr), tok)],
        out_shape=[jax.ShapeDtypeStruct((n, kvl), F32), jax.ShapeDtypeStruct((n, rr), F32),
                   jax.ShapeDtypeStruct((n, kvl), BF16), jax.ShapeDtypeStruct((n, rr), BF16)],
        compiler_params=_cparams("arbitrary"),
        name="shared_kv",
    )(h, g.reshape(1, d), wdkv, gc.reshape(1, kvl), wkr, wkrs, cos, sin)


def _mla_q_kernel(h_ref, g_ref, wdq_ref, gq_ref, wn_ref, wr_ref, wrs_ref, wuk_ref, cos_ref, sin_ref,
                  ql_ref, qr_ref, *, scale):
    xn = _rms(h_ref[...], g_ref[...]).astype(BF16)
    cq = _rms(jnp.dot(xn, wdq_ref[...], preferred_element_type=F32), gq_ref[...]).astype(BF16)
    qn = jnp.dot(cq, wn_ref[...], preferred_element_type=F32).astype(BF16)
    qr = (jnp.dot(cq, wr_ref[...], preferred_element_type=F32) * cos_ref[...]
          + jnp.dot(cq, wrs_ref[...], preferred_element_type=F32) * sin_ref[...])
    qr_ref[...] = (qr * scale).astype(BF16)
    npair, pw, ow = wuk_ref.shape
    for p in range(npair):
        ql = jnp.dot(qn[:, p * pw:(p + 1) * pw], wuk_ref[p], preferred_element_type=F32)
        ql_ref[:, p * ow:(p + 1) * ow] = (ql * scale).astype(BF16)


def _mla_q(h, g, wdq, gq, wn, wr, wrs, wuk_bd, cos, sin, *, scale, tn):
    n, d = h.shape
    ql_w = wuk_bd.shape[0] * wuk_bd.shape[2]
    qr_w = wr.shape[1]
    period = cos.shape[0] // tn
    tok = lambda i: (i, 0)
    return pl.pallas_call(
        functools.partial(_mla_q_kernel, scale=scale),
        grid=(n // tn,),
        in_specs=[pl.BlockSpec((tn, d), tok), _full((1, d)), _full(wdq.shape), _full((1, wdq.shape[1])),
                  _full(wn.shape), _full(wr.shape), _full(wrs.shape), _full(wuk_bd.shape),
                  pl.BlockSpec((tn, qr_w), lambda i: (i % period, 0)),
                  pl.BlockSpec((tn, qr_w), lambda i: (i % period, 0))],
        out_specs=[pl.BlockSpec((tn, ql_w), tok), pl.BlockSpec((tn, qr_w), tok)],
        out_shape=[jax.ShapeDtypeStruct((n, ql_w), BF16), jax.ShapeDtypeStruct((n, qr_w), BF16)],
        compiler_params=_cparams("arbitrary"),
        name="mla_q",
    )(h, g.reshape(1, d), wdq, gq.reshape(1, -1), wn, wr, wrs, wuk_bd, cos, sin)


def _softmax_step(s, c, m_s, l_s, acc_s):
    m_prev = m_s[...]
    m_new = jnp.maximum(m_prev, jnp.max(s, axis=1, keepdims=True))
    alpha = jnp.exp(m_prev - m_new)
    p = jnp.exp(s - m_new)
    l_s[...] = alpha * l_s[...] + jnp.sum(p, axis=1, keepdims=True)
    acc_s[...] = alpha * acc_s[...] + jnp.dot(p.astype(BF16), c, preferred_element_type=F32)
    m_s[...] = m_new


def _attn_prompt_kernel(ql_ref, qr_ref, c_ref, kr_ref, o_ref, m_s, l_s, acc_s, *, tq, tk, nheads):
    qi, ki = pl.program_id(1), pl.program_id(2)
    rows = tq * nheads
    q_lo, k_lo = qi * tq, ki * tk

    @pl.when(ki == 0)
    def _():
        m_s[...] = jnp.full(m_s.shape, NEG_INF, F32)
        l_s[...] = jnp.zeros(l_s.shape, F32)
        acc_s[...] = jnp.zeros(acc_s.shape, F32)

    def step(masked):
        c = c_ref[0]
        s = (lax.dot_general(ql_ref[0], c, _NT, preferred_element_type=F32)
             + lax.dot_general(qr_ref[0], kr_ref[0], _NT, preferred_element_type=F32))
        if masked:
            qpos = q_lo + lax.broadcasted_iota(jnp.int32, (rows, tk), 0) // nheads
            kpos = k_lo + lax.broadcasted_iota(jnp.int32, (rows, tk), 1)
            s = jnp.where(kpos <= qpos, s, NEG_INF)
        _softmax_step(s, c, m_s, l_s, acc_s)

    @pl.when(k_lo + tk - 1 <= q_lo)
    def _():
        step(False)

    @pl.when((k_lo + tk - 1 > q_lo) & (k_lo <= q_lo + tq - 1))
    def _():
        step(True)

    @pl.when(ki == pl.num_programs(2) - 1)
    def _():
        o_ref[0] = (acc_s[...] / l_s[...]).astype(BF16)


def _attn_prompt(ql, qr, cb, krb, *, nheads, tq=128, tk=512):
    b, th, kvl = ql.shape
    rr = qr.shape[2]
    t = th // nheads
    rows = tq * nheads
    kmap = lambda i, q, k: (i, jnp.minimum(k, (q * tq + tq - 1) // tk), 0)
    return pl.pallas_call(
        functools.partial(_attn_prompt_kernel, tq=tq, tk=tk, nheads=nheads),
        grid=(b, t // tq, t // tk),
        in_specs=[pl.BlockSpec((1, rows, kvl), lambda i, q, k: (i, q, 0)),
                  pl.BlockSpec((1, rows, rr), lambda i, q, k: (i, q, 0)),
                  pl.BlockSpec((1, tk, kvl), kmap), pl.BlockSpec((1, tk, rr), kmap)],
        out_specs=pl.BlockSpec((1, rows, kvl), lambda i, q, k: (i, q, 0)),
        out_shape=jax.ShapeDtypeStruct((b, th, kvl), BF16),
        scratch_shapes=[pltpu.VMEM((rows, 1), F32), pltpu.VMEM((rows, 1), F32), pltpu.VMEM((rows, kvl), F32)],
        compiler_params=_cparams("arbitrary", "arbitrary", "arbitrary"),
        name="attn_prompt",
    )(ql, qr, cb, krb)


def _attn_sample_kernel(pt_ref, ql_ref, qr_ref, cn_ref, krn_ref, *rest, npp, nheads, n_new):
    del pt_ref
    cpages, krpages = rest[:npp], rest[npp:2 * npp]
    o_ref, m_s, l_s, acc_s = rest[2 * npp:]
    g = pl.program_id(1)
    ql, qr = ql_ref[0], qr_ref[0]
    rows = ql.shape[0]

    @pl.when(g == 0)
    def _():
        m_s[...] = jnp.full(m_s.shape, NEG_INF, F32)
        l_s[...] = jnp.zeros(l_s.shape, F32)
        acc_s[...] = jnp.zeros(acc_s.shape, F32)
        c = cn_ref[0]
        s = (lax.dot_general(ql, c, _NT, preferred_element_type=F32)
             + lax.dot_general(qr, krn_ref[0], _NT, preferred_element_type=F32))
        qpos = lax.broadcasted_iota(jnp.int32, s.shape, 0) // nheads
        kpos = lax.broadcasted_iota(jnp.int32, s.shape, 1)
        s = jnp.where((kpos <= qpos) & (kpos < n_new), s, NEG_INF)
        _softmax_step(s, c, m_s, l_s, acc_s)

    cs = [r[0].astype(BF16) for r in cpages]
    s = jnp.concatenate(
        [lax.dot_general(ql, cs[j], _NT, preferred_element_type=F32)
         + lax.dot_general(qr, krpages[j][0].astype(BF16), _NT, preferred_element_type=F32)
         for j in range(npp)], axis=1)
    m_prev = m_s[...]
    m_new = jnp.maximum(m_prev, jnp.max(s, axis=1, keepdims=True))
    alpha = jnp.exp(m_prev - m_new)
    p = jnp.exp(s - m_new)
    l_s[...] = alpha * l_s[...] + jnp.sum(p, axis=1, keepdims=True)
    pb = p.astype(BF16)
    page = cs[0].shape[0]
    pv = jnp.dot(pb[:, 0:page], cs[0], preferred_element_type=F32)
    for j in range(1, npp):
        pv = pv + jnp.dot(pb[:, j * page:(j + 1) * page], cs[j], preferred_element_type=F32)
    acc_s[...] = alpha * acc_s[...] + pv
    m_s[...] = m_new

    @pl.when(g == pl.num_programs(1) - 1)
    def _():
        o_ref[0] = (acc_s[...] / l_s[...]).astype(BF16)


def _attn_sample(ql, qr, cn, krn, cache_c, cache_kr, page_table, *, nheads, n_new, npp=8):
    b, rows, kvl = ql.shape
    rr = qr.shape[2]
    page = cache_c.shape[1]
    n_pages = page_table.shape[1]
    assert n_pages % npp == 0 and cn.shape[1] == page
    bmap = lambda i, g, pt: (i, 0, 0)
    pmap = lambda j: (lambda i, g, pt: (pt[i, g * npp + j], 0, 0))
    grid_spec = pltpu.PrefetchScalarGridSpec(
        num_scalar_prefetch=1,
        grid=(b, n_pages // npp),
        in_specs=[pl.BlockSpec((1, rows, kvl), bmap), pl.BlockSpec((1, rows, rr), bmap),
                  pl.BlockSpec((1, page, kvl), bmap), pl.BlockSpec((1, page, rr), bmap)]
                 + [pl.BlockSpec((1, page, kvl), pmap(j)) for j in range(npp)]
                 + [pl.BlockSpec((1, page, rr), pmap(j)) for j in range(npp)],
        out_specs=pl.BlockSpec((1, rows, kvl), bmap),
        scratch_shapes=[pltpu.VMEM((rows, 1), F32), pltpu.VMEM((rows, 1), F32), pltpu.VMEM((rows, kvl), F32)],
    )
    return pl.pallas_call(
        functools.partial(_attn_sample_kernel, npp=npp, nheads=nheads, n_new=n_new),
        grid_spec=grid_spec,
        out_shape=jax.ShapeDtypeStruct((b, rows, kvl), BF16),
        compiler_params=_cparams("arbitrary", "arbitrary"),
        name="attn_sample",
    )(page_table, ql, qr, cn, krn, *([cache_c] * npp), *([cache_kr] * npp))


def _mla_out_kernel(h_ref, o_ref, wuv_ref, wo_ref, y_ref):
    npair, pw, _ = wuv_ref.shape
    parts = [jnp.dot(o_ref[:, p * pw:(p + 1) * pw], wuv_ref[p], preferred_element_type=F32).astype(BF16)
             for p in range(npair)]
    y_ref[...] = h_ref[...] + jnp.dot(jnp.concatenate(parts, axis=-1), wo_ref[...], preferred_element_type=F32)


def _mla_out(h, o, wuv_bd, wo, *, tn):
    n, d = h.shape
    tok = lambda i: (i, 0)
    return pl.pallas_call(
        _mla_out_kernel,
        grid=(n // tn,),
        in_specs=[pl.BlockSpec((tn, d), tok), pl.BlockSpec((tn, o.shape[1]), tok),
                  _full(wuv_bd.shape), _full(wo.shape)],
        out_specs=pl.BlockSpec((tn, d), tok),
        out_shape=jax.ShapeDtypeStruct((n, d), F32),
        compiler_params=_cparams("arbitrary"),
        name="mla_out",
    )(h, o, wuv_bd, wo)


def _rope_tables(pos, half, reps):
    inv = ROPE_THETA ** (-jnp.arange(half, dtype=F32) / half)
    ang = pos.astype(F32)[:, None] * inv[None, :]
    cos, sin = jnp.cos(ang), jnp.sin(ang)
    return (jnp.tile(jnp.concatenate([cos, cos], axis=-1), (1, reps)),
            jnp.tile(jnp.concatenate([-sin, sin], axis=-1), (1, reps)))


def _swap_halves(w):
    half = w.shape[-1] // 2
    return jnp.concatenate([w[..., half:], w[..., :half]], axis=-1)


def _pair_block_diag(w):
    h, k, n = w.shape
    z = jnp.zeros((h // 2, k, n), w.dtype)
    top = jnp.concatenate([w[0::2], z], axis=2)
    bot = jnp.concatenate([z, w[1::2]], axis=2)
    return jnp.concatenate([top, bot], axis=1)


def kernel(x_prompt, x_sample, cache_ckv, cache_krope, page_table, state_h, state_conv, ln_mix, ln_ffn, ln_final, lru_w_in, lru_conv_w, lru_conv_b, lru_gate_a_w, lru_gate_a_b, lru_gate_x_w, lru_gate_x_b, lru_lambda, lru_w_out, kv_norm, w_dkv, ckv_norm, w_kr, w_uk, w_uv, mla_w_dq, mla_q_norm, mla_w_uq, mla_w_o, peer_w_q, peer_subkeys, peer_u, peer_v):
    bp, sp, d = x_prompt.shape
    bd, sd, _ = x_sample.shape
    depth = ln_mix.shape[0]
    assert depth == 2 and lru_w_in.shape[0] == 1 and mla_w_dq.shape[0] == 1
    kvl, nheads, qk_nope = w_uk.shape
    v_head = w_uv.shape[2]
    qk_rope = w_kr.shape[1]
    page = cache_ckv.shape[1]
    past_len = page_table.shape[1] * page
    scale = float(qk_nope + qk_rope) ** -0.5
    np_tok, nd_tok = bp * sp, bd * sd

    lw = (ln_mix[0], lru_w_in[0].astype(BF16), lru_conv_w[0], lru_conv_b[0],
          lru_gate_a_w[0].astype(BF16), lru_gate_a_b[0], lru_gate_x_w[0].astype(BF16), lru_gate_x_b[0],
          lru_lambda[0], lru_w_out[0].astype(BF16))
    hp, hl_p, cb_p = _lru_prompt(x_prompt, *lw)
    hd, hl_d, cb_d = _lru_sample(x_sample, state_conv[:, 0], state_h[:, 0], *lw)
    hp = hp.reshape(np_tok, d)
    hd = hd.reshape(nd_tok, d)

    def peer(h, l, final_norm):
        return _peer(h, ln_ffn[l], peer_w_q[l].astype(BF16), peer_subkeys[l].astype(BF16),
                     peer_u[l].astype(BF16), peer_v[l].T.astype(BF16), ln_final, final_norm=final_norm)

    hp = peer(hp, 0, False)
    hd = peer(hd, 0, False)

    pos_p = jnp.arange(sp, dtype=jnp.int32)
    pos_d = past_len + jnp.arange(sd, dtype=jnp.int32)
    kvw = (kv_norm, w_dkv.astype(BF16), ckv_norm, w_kr.astype(BF16), _swap_halves(w_kr).astype(BF16))
    tn_p, tn_d = 512, nd_tok
    cos_p, sin_p = _rope_tables(pos_p, qk_rope // 2, 1)
    cos_d, sin_d = _rope_tables(jnp.tile(pos_d, bd), qk_rope // 2, 1)
    c_p, kr_p, cb16_p, krb16_p = _shared_kv(hp, *kvw, cos_p, sin_p, tn=tn_p)
    c_d, kr_d, cb16_d, krb16_d = _shared_kv(hd, *kvw, cos_d, sin_d, tn=tn_d)

    wuq = mla_w_uq[0].reshape(-1, nheads, qk_nope + qk_rope)
    w_nope = wuq[:, :, :qk_nope].reshape(-1, nheads * qk_nope).astype(BF16)
    w_rope = wuq[:, :, qk_nope:]
    w_rope_sw = _swap_halves(w_rope).reshape(-1, nheads * qk_rope).astype(BF16)
    w_rope = w_rope.reshape(-1, nheads * qk_rope).astype(BF16)
    wuk_bd = _pair_block_diag(jnp.transpose(w_uk, (1, 2, 0))).astype(BF16)
    wuv_bd = _pair_block_diag(jnp.transpose(w_uv, (1, 0, 2))).astype(BF16)
    qw = (ln_mix[1], mla_w_dq[0].astype(BF16), mla_q_norm[0], w_nope, w_rope, w_rope_sw, wuk_bd)
    ql_p, qr_p = _mla_q(hp, *qw, jnp.tile(cos_p, (1, nheads)), jnp.tile(sin_p, (1, nheads)), scale=scale, tn=tn_p)
    ql_d, qr_d = _mla_q(hd, *qw, jnp.tile(cos_d, (1, nheads)), jnp.tile(sin_d, (1, nheads)), scale=scale, tn=tn_d)

    o_p = _attn_prompt(ql_p.reshape(bp, sp * nheads, kvl), qr_p.reshape(bp, sp * nheads, qk_rope),
                       cb16_p.reshape(bp, sp, kvl), krb16_p.reshape(bp, sp, qk_rope), nheads=nheads)
    pad = ((0, 0), (0, page - sd), (0, 0))
    o_d = _attn_sample(ql_d.reshape(bd, sd * nheads, kvl), qr_d.reshape(bd, sd * nheads, qk_rope),
                       jnp.pad(cb16_d.reshape(bd, sd, kvl), pad), jnp.pad(krb16_d.reshape(bd, sd, qk_rope), pad),
                       cache_ckv, cache_krope, page_table, nheads=nheads, n_new=sd)
    wo = mla_w_o[0].astype(BF16)
    hp = _mla_out(hp, o_p.reshape(np_tok, nheads * kvl), wuv_bd, wo, tn=tn_p)
    hd = _mla_out(hd, o_d.reshape(nd_tok, nheads * kvl), wuv_bd, wo, tn=tn_d)

    y_p = peer(hp, 1, True)
    y_d = peer(hd, 1, True)

    return (y_p.reshape(bp, sp, d), y_d.reshape(bd, sd, d),
            hl_p, cb_p[:, None], c_p.reshape(bp, sp, kvl), kr_p.reshape(bp, sp, qk_rope),
            hl_d[:, None], cb_d[:, None], c_d.reshape(bd, sd, kvl), kr_d.reshape(bd, sd, qk_rope))
```

```python
import functools

import jax
import jax.numpy as jnp
from jax import lax
from jax.experimental import pallas as pl
from jax.experimental.pallas import tpu as pltpu

F32 = jnp.float32
BF16 = jnp.bfloat16

RMS_EPS = 1e-6
NEG_INF = -1e30
LRU_C = 8.0
CONV_W = 4
ROPE_THETA = 10000.0
PEER_TOPK = 16

LANES = 128
SUBLANES = 8
VMEM_LIMIT = 56 * 1024 * 1024

_NT = (((1,), (1,)), ((), ()))


def _cparams(*sem):
    return pltpu.CompilerParams(dimension_semantics=sem, vmem_limit_bytes=VMEM_LIMIT)


def _rms(x, g):
    return x * lax.rsqrt(jnp.mean(x * x, axis=-1, keepdims=True) + RMS_EPS) * g


def _gelu(x):
    inner = x * (0.7978845608028654 + (0.7978845608028654 * 0.044715) * (x * x))
    return x * (0.5 + 0.5 * jnp.tanh(inner))


def _sigmoid(x):
    return 1.0 / (1.0 + jnp.exp(-x))


def _softplus(x):
    return jnp.maximum(x, 0.0) + jnp.log(1.0 + jnp.exp(-jnp.abs(x)))


def _full(shape):
    n = len(shape)
    return pl.BlockSpec(shape, lambda *_: (0,) * n)


def _lru_gates(xc, gaw_ref, gab_ref, gxw_ref, gxb_ref, lam_ref):
    nh, bw, _ = gaw_ref.shape
    xcb = xc.astype(BF16)
    r_parts, i_parts = [], []
    for h in range(nh):
        blk = xcb[:, h * bw:(h + 1) * bw]
        r_parts.append(jnp.dot(blk, gaw_ref[h], preferred_element_type=F32))
        i_parts.append(jnp.dot(blk, gxw_ref[h], preferred_element_type=F32))
    r = _sigmoid(jnp.concatenate(r_parts, axis=-1) + gab_ref[...])
    i = _sigmoid(jnp.concatenate(i_parts, axis=-1) + gxb_ref[...])
    log_a = (-LRU_C) * r * _softplus(-lam_ref[...])
    a = jnp.exp(log_a)
    bx = jnp.sqrt(1.0 - jnp.exp(2.0 * log_a)) * i * xc
    return a, bx


def _lru_prompt_kernel(x_ref, g_ref, win_ref, cw_ref, cb_ref, gaw_ref, gab_ref, gxw_ref, gxb_ref,
                       lam_ref, wout_ref, y_ref, hl_ref, cbuf_ref, xcat, a_s, b_s, h_s, hcar):
    t = pl.program_id(1)
    tt, d = x_ref.shape[1], x_ref.shape[2]

    @pl.when(t == 0)
    def _():
        xcat[0:SUBLANES, :] = jnp.zeros((SUBLANES, d), F32)
        hcar[...] = jnp.zeros((1, d), F32)

    x = x_ref[0]
    xn = _rms(x, g_ref[...])
    proj = jnp.dot(xn.astype(BF16), win_ref[...], preferred_element_type=F32)
    gate = _gelu(proj[:, :d])
    xb = proj[:, d:]
    xcat[SUBLANES:SUBLANES + tt, :] = xb
    xc = (cb_ref[...]
          + cw_ref[0:1, :] * xcat[SUBLANES - 3:SUBLANES - 3 + tt, :]
          + cw_ref[1:2, :] * xcat[SUBLANES - 2:SUBLANES - 2 + tt, :]
          + cw_ref[2:3, :] * xcat[SUBLANES - 1:SUBLANES - 1 + tt, :]
          + cw_ref[3:4, :] * xb)
    tail = xcat[tt:tt + SUBLANES, :]
    xcat[0:SUBLANES, :] = tail
    cbuf_ref[0] = tail

    a, bx = _lru_gates(xc, gaw_ref, gab_ref, gxw_ref, gxb_ref, lam_ref)
    a_s[...] = a
    b_s[...] = bx

    def body(i, h):
        h = a_s[pl.ds(i, 1), :] * h + b_s[pl.ds(i, 1), :]
        h_s[pl.ds(i, 1), :] = h
        return h

    h = lax.fori_loop(0, tt, body, hcar[...], unroll=8)
    hcar[...] = h
    hl_ref[0] = h
    y = jnp.dot((h_s[...] * gate).astype(BF16), wout_ref[...], preferred_element_type=F32)
    y_ref[0] = x + y


def _lru_prompt(x, g, win, cw, cb, gaw, gab, gxw, gxb, lam, wout, *, tt=256):
    b, t, d = x.shape
    nh, bw, _ = gaw.shape
    row = lambda v: v.reshape(1, -1)
    wspecs = [_full((1, d)), _full((d, 2 * d)), _full((CONV_W, d)), _full((1, d)),
              _full((nh, bw, bw)), _full((1, d)), _full((nh, bw, bw)), _full((1, d)),
              _full((1, d)), _full((d, d))]
    y, hl, cbuf = pl.pallas_call(
        _lru_prompt_kernel,
        grid=(b, t // tt),
        in_specs=[pl.BlockSpec((1, tt, d), lambda i, j: (i, j, 0))] + wspecs,
        out_specs=[pl.BlockSpec((1, tt, d), lambda i, j: (i, j, 0)),
                   pl.BlockSpec((1, 1, d), lambda i, j: (i, 0, 0)),
                   pl.BlockSpec((1, SUBLANES, d), lambda i, j: (i, 0, 0))],
        out_shape=[jax.ShapeDtypeStruct((b, t, d), F32),
                   jax.ShapeDtypeStruct((b, 1, d), F32),
                   jax.ShapeDtypeStruct((b, SUBLANES, d), F32)],
        scratch_shapes=[pltpu.VMEM((tt + SUBLANES, d), F32), pltpu.VMEM((tt, d), F32),
                        pltpu.VMEM((tt, d), F32), pltpu.VMEM((tt, d), F32), pltpu.VMEM((1, d), F32)],
        compiler_params=_cparams("arbitrary", "arbitrary"),
        name="lru_prompt",
    )(x, row(g), win, cw, row(cb), gaw, row(gab), gxw, row(gxb), row(lam), wout)
    return y, hl, cbuf[:, SUBLANES - (CONV_W - 1):, :]


def _lru_sample_kernel(x_ref, conv_ref, h0_ref, g_ref, win_ref, cw_ref, cb_ref, gaw_ref, gab_ref,
                       gxw_ref, gxb_ref, lam_ref, wout_ref, y_ref, hl_ref, cbuf_ref):
    t, b, d = x_ref.shape
    x = x_ref[...].reshape(t * b, d)
    xn = _rms(x, g_ref[...])
    proj = jnp.dot(xn.astype(BF16), win_ref[...], preferred_element_type=F32)
    gate = _gelu(proj[:, :d])
    xb = proj[:, d:]
    seq = [conv_ref[k] for k in range(CONV_W - 1)] + [xb[k * b:(k + 1) * b, :] for k in range(t)]
    xc_parts = []
    for k in range(t):
        acc = cb_ref[...] + cw_ref[0:1, :] * seq[k]
        for j in range(1, CONV_W):
            acc = acc + cw_ref[j:j + 1, :] * seq[k + j]
        xc_parts.append(acc)
    for k in range(CONV_W - 1):
        cbuf_ref[k] = seq[t + k]
    xc = jnp.concatenate(xc_parts, axis=0)
    a, bx = _lru_gates(xc, gaw_ref, gab_ref, gxw_ref, gxb_ref, lam_ref)
    h = h0_ref[...]
    hs = []
    for k in range(t):
        h = a[k * b:(k + 1) * b, :] * h + bx[k * b:(k + 1) * b, :]
        hs.append(h)
    hl_ref[...] = h
    hall = jnp.concatenate(hs, axis=0)
    y = jnp.dot((hall * gate).astype(BF16), wout_ref[...], preferred_element_type=F32)
    y_ref[...] = (x + y).reshape(t, b, d)


def _lru_sample(x, conv, h0, g, win, cw, cb, gaw, gab, gxw, gxb, lam, wout):
    b, t, d = x.shape
    assert t >= CONV_W - 1
    row = lambda v: v.reshape(1, -1)
    y, hl, cbuf = pl.pallas_call(
        _lru_sample_kernel,
        out_shape=[jax.ShapeDtypeStruct((t, b, d), F32),
                   jax.ShapeDtypeStruct((b, d), F32),
                   jax.ShapeDtypeStruct((CONV_W - 1, b, d), F32)],
        compiler_params=pltpu.CompilerParams(vmem_limit_bytes=VMEM_LIMIT),
        name="lru_sample",
    )(jnp.swapaxes(x, 0, 1), jnp.swapaxes(conv, 0, 1), h0, row(g), win, cw, row(cb),
      gaw, row(gab), gxw, row(gxb), row(lam), wout)
    return jnp.swapaxes(y, 0, 1), hl, jnp.swapaxes(cbuf, 0, 1)


def _cand_pairs(k):
    return [(a, min(k, (k + 1) // (a + 1))) for a in range(k) if (k + 1) // (a + 1) >= 1]


def _peer_route_kernel(h_ref, g_ref, wq_ref, sk_ref, xnt_ref, s1c_ref, eb_ref, t0_ref, aw_ref,
                       q_s, st_s, sv_s, cand_s):
    tn, d = h_ref.shape
    nh, _, nk, dk2 = sk_ref.shape
    xn = _rms(h_ref[...], g_ref[...])
    xnt = xn.T.astype(BF16)
    tw = xnt_ref.shape[2]
    for k in range(xnt_ref.shape[0]):
        xnt_ref[k] = xnt[:, k * tw:(k + 1) * tw]
    q = jnp.dot(xn.astype(BF16), wq_ref[...], preferred_element_type=F32).astype(BF16)
    for hp in range(2 * nh):
        q_s[hp] = q[:, hp * dk2:(hp + 1) * dk2]

    def scores(hp, carry):
        s = lax.dot_general(sk_ref[hp // 2, hp % 2], q_s[hp], _NT, preferred_element_type=F32)
        st_s[hp] = s
        for ts in range(tn // LANES):
            cur = s[:, ts * LANES:(ts + 1) * LANES]
            for r in range(PEER_TOPK):
                m = jnp.max(cur, axis=0, keepdims=True)
                sv_s[hp, r:r + 1, ts * LANES:(ts + 1) * LANES] = m
                cur = jnp.where(cur == m, -jnp.inf, cur)
        return carry

    lax.fori_loop(0, 2 * nh, scores, 0)

    pairs = _cand_pairs(PEER_TOPK)
    ncand = sum(nb for _, nb in pairs)
    cand_s[...] = jnp.full(cand_s.shape, -jnp.inf, F32)

    def thresholds(h, carry):
        for ts in range(tn // LANES):
            sl = slice(ts * LANES, (ts + 1) * LANES)
            sv0 = sv_s[2 * h, :, sl]
            sv1 = sv_s[2 * h + 1, :, sl]
            off = 0
            for a, nb in pairs:
                cand_s[off:off + nb, :] = sv0[a:a + 1, :] + sv1[0:nb, :]
                off += nb
            cand0 = cand_s[...]
            cur = cand0
            t_hi = None
            for r in range(PEER_TOPK + 1):
                m = jnp.max(cur, axis=0, keepdims=True)
                if r == PEER_TOPK - 1:
                    t_hi = m
                cur = jnp.where(cur == m, -jnp.inf, cur)
            t_mid = 0.5 * (t_hi + m)
            m0 = sv0[0:1, :]
            m1 = sv1[0:1, :]
            z = jnp.sum(jnp.where(cand0 >= t_mid, jnp.exp(cand0 - (m0 + m1)), 0.0), axis=0, keepdims=True)
            s0 = st_s[2 * h, :, sl]
            s1c = st_s[2 * h + 1, :, sl] - m1
            s1c_ref[h, ts] = s1c
            eb_ref[h, ts] = jnp.exp(s1c)
            t0_ref[h, ts] = (t_mid - m1) - s0
            aw_ref[h, ts] = jnp.exp(s0 - m0) * (1.0 / z)
        return carry

    assert ncand <= cand_s.shape[0]
    lax.fori_loop(0, nh, thresholds, 0)


def _row_bcast(ref, h, ts, i):
    return ref[h, ts, pl.ds(i, SUBLANES, stride=0), :]


def _peer_dense_kernel(h_ref, xnt_ref, u_ref, vt_ref, s1c_ref, eb_ref, t0_ref, aw_ref, gf_ref, o_ref,
                       act_a, act_b, p_a, p_b, acc_s, *, final_norm):
    c = pl.program_id(1)
    nh, _, ic, _ = t0_ref.shape
    nk = s1c_ref.shape[2]
    ntw, _, tw = xnt_ref.shape
    nrg, _, rb = vt_ref.shape
    rg = rb // nk
    jb = min(64, nk)
    n_it = ntw * nrg

    @pl.when(c == 0)
    def _():
        acc_s[...] = jnp.zeros(acc_s.shape, F32)

    def act_dot(it):
        rows = pl.ds(pl.multiple_of((it % nrg) * rb, rb), rb)
        return jnp.dot(u_ref[rows, :], xnt_ref[it // nrg], preferred_element_type=F32)

    def out_dot(it, p_ref):
        acc_s[it // nrg] += jnp.dot(vt_ref[it % nrg], p_ref[...], preferred_element_type=F32)

    def gates(it, act_ref, p_ref):
        i0 = (it % nrg) * rg
        for tsub in range(tw // LANES):
            ts = (it // nrg) * (tw // LANES) + tsub
            sl = slice(tsub * LANES, (tsub + 1) * LANES)
            for j0 in range(0, nk, jb):
                gs = [jnp.zeros((jb // SUBLANES, SUBLANES, LANES), F32) for _ in range(rg)]
                for h in range(nh):
                    s1c = s1c_ref[h, ts, j0:j0 + jb, :].reshape(jb // SUBLANES, SUBLANES, LANES)
                    eb = eb_ref[h, ts, j0:j0 + jb, :].reshape(jb // SUBLANES, SUBLANES, LANES)
                    for r in range(rg):
                        sel = jnp.where(s1c >= _row_bcast(t0_ref, h, ts, i0 + r), eb, 0.0)
                        gs[r] = gs[r] + sel * _row_bcast(aw_ref, h, ts, i0 + r)
                for r in range(rg):
                    es = slice(r * nk + j0, r * nk + j0 + jb)
                    p_ref[es, sl] = (_gelu(act_ref[es, sl]) * gs[r].reshape(jb, LANES)).astype(BF16)

    act_a[...] = act_dot(0)
    p_b[...] = jnp.zeros(p_b.shape, BF16)

    def item_pair(k, carry):
        it = 2 * k
        act_b[...] = act_dot(it + 1)
        gates(it, act_a, p_a)
        out_dot(jnp.maximum(it - 1, 0), p_b)
        act_a[...] = act_dot(jnp.minimum(it + 2, n_it - 1))
        gates(it + 1, act_b, p_b)
        out_dot(it, p_a)
        return carry

    assert n_it % 2 == 0
    lax.fori_loop(0, n_it // 2, item_pair, 0)
    out_dot(n_it - 1, p_b)

    @pl.when(c == pl.num_programs(1) - 1)
    def _():
        out = h_ref[...] + jnp.concatenate([acc_s[k].T for k in range(ntw)], axis=0)
        if final_norm:
            out = _rms(out, gf_ref[...])
        o_ref[...] = out


def _peer(h, g, wq, sk, u, vt, gf, *, final_norm, tn=512, tw=256, ec=2048):
    n, d = h.shape
    nh, _, nk, dk2 = sk.shape
    ne = u.shape[0]
    rb = vt.shape[2]
    ec = min(ec, ne)
    assert ne == nk * nk and n % tn == 0 and ne % ec == 0 and ec % (nk * SUBLANES) == 0
    assert tn % tw == 0 and tw % LANES == 0 and rb % nk == 0 and ec % rb == 0
    ic = ec // nk
    ncand_pad = 56
    nts = tn // LANES
    rt_spec = pl.BlockSpec((nh, nts, nk, LANES), lambda i: (0, i, 0, 0))
    rt_shape = jax.ShapeDtypeStruct((nh, n // LANES, nk, LANES), F32)
    xnt, s1c, eb, t0, aw = pl.pallas_call(
        _peer_route_kernel,
        grid=(n // tn,),
        in_specs=[pl.BlockSpec((tn, d), lambda i: (i, 0)), _full((1, d)), _full(wq.shape), _full(sk.shape)],
        out_specs=[pl.BlockSpec((tn // tw, d, tw), lambda i: (i, 0, 0))] + [rt_spec] * 4,
        out_shape=[jax.ShapeDtypeStruct((n // tw, d, tw), BF16)] + [rt_shape] * 4,
        scratch_shapes=[pltpu.VMEM((2 * nh, tn, dk2), BF16), pltpu.VMEM((2 * nh, nk, tn), F32),
                        pltpu.VMEM((2 * nh, PEER_TOPK, tn), F32), pltpu.VMEM((ncand_pad, LANES), F32)],
        compiler_params=_cparams("arbitrary"),
        name="peer_route",
    )(h, g.reshape(1, d), wq, sk)
    return pl.pallas_call(
        functools.partial(_peer_dense_kernel, final_norm=final_norm),
        grid=(n // tn, ne // ec),
        in_specs=[pl.BlockSpec((tn, d), lambda i, c: (i, 0)),
                  pl.BlockSpec((tn // tw, d, tw), lambda i, c: (i, 0, 0)),
                  pl.BlockSpec((ec, d), lambda i, c: (c, 0)),
                  pl.BlockSpec((ec // rb, d, rb), lambda i, c: (c, 0, 0)),
                  pl.BlockSpec((nh, nts, nk, LANES), lambda i, c: (0, i, 0, 0)),
                  pl.BlockSpec((nh, nts, nk, LANES), lambda i, c: (0, i, 0, 0)),
                  pl.BlockSpec((nh, nts, ic, LANES), lambda i, c: (0, i, c, 0)),
                  pl.BlockSpec((nh, nts, ic, LANES), lambda i, c: (0, i, c, 0)),
                  _full((1, d))],
        out_specs=pl.BlockSpec((tn, d), lambda i, c: (i, 0)),
        out_shape=jax.ShapeDtypeStruct((n, d), F32),
        scratch_shapes=[pltpu.VMEM((rb, tw), F32), pltpu.VMEM((rb, tw), F32),
                        pltpu.VMEM((rb, tw), BF16), pltpu.VMEM((rb, tw), BF16),
                        pltpu.VMEM((tn // tw, d, tw), F32)],
        compiler_params=_cparams("arbitrary", "arbitrary"),
        name="peer_dense",
    )(h, xnt, u, vt, s1c, eb, t0, aw, gf.reshape(1, d))


def _shared_kv_kernel(h_ref, g_ref, wdkv_ref, gc_ref, wkr_ref, wkrs_ref, cos_ref, sin_ref,
                      c_ref, kr_ref, cb_ref, krb_ref):
    hn = _rms(h_ref[...], g_ref[...]).astype(BF16)
    c = _rms(jnp.dot(hn, wdkv_ref[...], preferred_element_type=F32), gc_ref[...])
    kr = (jnp.dot(hn, wkr_ref[...], preferred_element_type=F32) * cos_ref[...]
          + jnp.dot(hn, wkrs_ref[...], preferred_element_type=F32) * sin_ref[...])
    c_ref[...] = c
    kr_ref[...] = kr
    cb_ref[...] = c.astype(BF16)
    krb_ref[...] = kr.astype(BF16)


def _shared_kv(h, g, wdkv, gc, wkr, wkrs, cos, sin, *, tn):
    n, d = h.shape
    kvl, rr = wdkv.shape[1], wkr.shape[1]
    period = cos.shape[0] // tn
    tok = lambda i: (i, 0)
    return pl.pallas_call(
        _shared_kv_kernel,
        grid=(n // tn,),
        in_specs=[pl.BlockSpec((tn, d), tok), _full((1, d)), _full(wdkv.shape), _full((1, kvl)),
                  _full(wkr.shape), _full(wkrs.shape),
                  pl.BlockSpec((tn, rr), lambda i: (i % period, 0)),
                  pl.BlockSpec((tn, rr), lambda i: (i % period, 0))],
        out_specs=[pl.BlockSpec((tn, kvl), tok), pl.BlockSpec((tn, rr), tok),
                   pl.BlockSpec((tn, kvl), tok), pl.BlockSpec((tn, rr), tok)],
        out_shape=[jax.ShapeDtypeStruct((n, kvl), F32), jax.ShapeDtypeStruct((n, rr), F32),
                   jax.ShapeDtypeStruct((n, kvl), BF16), jax.ShapeDtypeStruct((n, rr), BF16)],
        compiler_params=_cparams("arbitrary"),
        name="shared_kv",
    )(h, g.reshape(1, d), wdkv, gc.reshape(1, kvl), wkr, wkrs, cos, sin)


def _mla_q_kernel(h_ref, g_ref, wdq_ref, gq_ref, wn_ref, wr_ref, wrs_ref, wuk_ref, cos_ref, sin_ref,
                  ql_ref, qr_ref, *, scale):
    xn = _rms(h_ref[...], g_ref[...]).astype(BF16)
    cq = _rms(jnp.dot(xn, wdq_ref[...], preferred_element_type=F32), gq_ref[...]).astype(BF16)
    qn = jnp.dot(cq, wn_ref[...], preferred_element_type=F32).astype(BF16)
    qr = (jnp.dot(cq, wr_ref[...], preferred_element_type=F32) * cos_ref[...]
          + jnp.dot(cq, wrs_ref[...], preferred_element_type=F32) * sin_ref[...])
    qr = (qr * scale).astype(BF16)
    nheads, _, rr = qr_ref.shape
    for h in range(nheads):
        qr_ref[h] = qr[:, h * rr:(h + 1) * rr]
    npair, pw, ow = wuk_ref.shape
    for p in range(npair):
        ql = jnp.dot(qn[:, p * pw:(p + 1) * pw], wuk_ref[p], preferred_element_type=F32)
        ql = (ql * scale).astype(BF16)
        ql_ref[2 * p] = ql[:, :ow // 2]
        ql_ref[2 * p + 1] = ql[:, ow // 2:]


def _mla_q(h, g, wdq, gq, wn, wr, wrs, wuk_bd, cos, sin, *, scale, tn):
    n, d = h.shape
    nheads = 2 * wuk_bd.shape[0]
    kvl = wuk_bd.shape[2] // 2
    qr_w = wr.shape[1]
    rr = qr_w // nheads
    period = cos.shape[0] // tn
    tok = lambda i: (i, 0)
    htok = lambda i: (0, i, 0)
    return pl.pallas_call(
        functools.partial(_mla_q_kernel, scale=scale),
        grid=(n // tn,),
        in_specs=[pl.BlockSpec((tn, d), tok), _full((1, d)), _full(wdq.shape), _full((1, wdq.shape[1])),
                  _full(wn.shape), _full(wr.shape), _full(wrs.shape), _full(wuk_bd.shape),
                  pl.BlockSpec((tn, qr_w), lambda i: (i % period, 0)),
                  pl.BlockSpec((tn, qr_w), lambda i: (i % period, 0))],
        out_specs=[pl.BlockSpec((nheads, tn, kvl), htok), pl.BlockSpec((nheads, tn, rr), htok)],
        out_shape=[jax.ShapeDtypeStruct((nheads, n, kvl), BF16), jax.ShapeDtypeStruct((nheads, n, rr), BF16)],
        compiler_params=_cparams("arbitrary"),
        name="mla_q",
    )(h, g.reshape(1, d), wdq, gq.reshape(1, -1), wn, wr, wrs, wuk_bd, cos, sin)


def _softmax_step(s, c, m_s, l_s, acc_s, rows=slice(None)):
    m_prev = m_s[rows]
    m_new = jnp.maximum(m_prev, jnp.max(s, axis=1, keepdims=True))
    alpha = jnp.exp(m_prev - m_new)
    p = jnp.exp(s - m_new)
    l_s[rows] = alpha * l_s[rows] + jnp.sum(p, axis=1, keepdims=True)
    acc_s[rows] = alpha * acc_s[rows] + jnp.dot(p.astype(BF16), c, preferred_element_type=F32)
    m_s[rows] = m_new


def _attn_prompt_kernel(ql_ref, qr_ref, c_ref, kr_ref, o_ref, m_s, l_s, acc_s, *, tq, tk, hc):
    qi, ki = pl.program_id(1), pl.program_id(2)
    nheads, _, kvl = ql_ref.shape
    rr = qr_ref.shape[2]
    q_lo, k_lo = qi * tq, ki * tk

    @pl.when(ki == 0)
    def _():
        m_s[...] = jnp.full(m_s.shape, NEG_INF, F32)
        l_s[...] = jnp.zeros(l_s.shape, F32)
        acc_s[...] = jnp.zeros(acc_s.shape, F32)

    def step(masked):
        c, kr = c_ref[0], kr_ref[0]
        rows = hc * tq
        for h0 in range(0, nheads, hc):
            ql = ql_ref[h0:h0 + hc].reshape(rows, kvl)
            qr = qr_ref[h0:h0 + hc].reshape(rows, rr)
            s = (lax.dot_general(ql, c, _NT, preferred_element_type=F32)
                 + lax.dot_general(qr, kr, _NT, preferred_element_type=F32))
            if masked:
                qpos = q_lo + (lax.broadcasted_iota(jnp.int32, (rows, tk), 0) & (tq - 1))
                kpos = k_lo + lax.broadcasted_iota(jnp.int32, (rows, tk), 1)
                s = jnp.where(kpos <= qpos, s, NEG_INF)
            _softmax_step(s, c, m_s, l_s, acc_s, slice(h0 * tq, (h0 + hc) * tq))

    @pl.when(k_lo + tk - 1 <= q_lo)
    def _():
        step(False)

    @pl.when((k_lo + tk - 1 > q_lo) & (k_lo <= q_lo + tq - 1))
    def _():
        step(True)

    @pl.when(ki == pl.num_programs(2) - 1)
    def _():
        o_ref[...] = (acc_s[...] / l_s[...]).astype(BF16).reshape(nheads, tq, kvl)


def _attn_prompt(ql, qr, cb, krb, *, tq=128, tk=512, hc=4):
    nheads, n, kvl = ql.shape
    rr = qr.shape[2]
    b, t, _ = cb.shape
    assert tq & (tq - 1) == 0 and t % tq == 0 and t % tk == 0 and nheads % hc == 0
    nq = t // tq
    rows = tq * nheads
    qmap = lambda i, q, k: (0, i * nq + q, 0)
    kmap = lambda i, q, k: (i, jnp.minimum(k, (q * tq + tq - 1) // tk), 0)
    return pl.pallas_call(
        functools.partial(_attn_prompt_kernel, tq=tq, tk=tk, hc=hc),
        grid=(b, nq, t // tk),
        in_specs=[pl.BlockSpec((nheads, tq, kvl), qmap), pl.BlockSpec((nheads, tq, rr), qmap),
                  pl.BlockSpec((1, tk, kvl), kmap), pl.BlockSpec((1, tk, rr), kmap)],
        out_specs=pl.BlockSpec((nheads, tq, kvl), qmap),
        out_shape=jax.ShapeDtypeStruct((nheads, n, kvl), BF16),
        scratch_shapes=[pltpu.VMEM((rows, 1), F32), pltpu.VMEM((rows, 1), F32), pltpu.VMEM((rows, kvl), F32)],
        compiler_params=_cparams("arbitrary", "arbitrary", "arbitrary"),
        name="attn_prompt",
    )(ql, qr, cb, krb)


def _attn_sample_kernel(pt_ref, ql_ref, qr_ref, cn_ref, krn_ref, *rest, npp, nheads, n_new):
    del pt_ref
    cpages, krpages = rest[:npp], rest[npp:2 * npp]
    o_ref, m_s, l_s, acc_s = rest[2 * npp:]
    g = pl.program_id(1)
    ql, qr = ql_ref[0], qr_ref[0]
    rows = ql.shape[0]

    @pl.when(g == 0)
    def _():
        m_s[...] = jnp.full(m_s.shape, NEG_INF, F32)
        l_s[...] = jnp.zeros(l_s.shape, F32)
        acc_s[...] = jnp.zeros(acc_s.shape, F32)
        c = cn_ref[0]
        s = (lax.dot_general(ql, c, _NT, preferred_element_type=F32)
             + lax.dot_general(qr, krn_ref[0], _NT, preferred_element_type=F32))
        qpos = lax.rem(lax.broadcasted_iota(jnp.int32, s.shape, 0), n_new)
        kpos = lax.broadcasted_iota(jnp.int32, s.shape, 1)
        s = jnp.where((kpos <= qpos) & (kpos < n_new), s, NEG_INF)
        _softmax_step(s, c, m_s, l_s, acc_s)

    cs = [r[0].astype(BF16) for r in cpages]
    s = jnp.concatenate(
        [lax.dot_general(ql, cs[j], _NT, preferred_element_type=F32)
         + lax.dot_general(qr, krpages[j][0].astype(BF16), _NT, preferred_element_type=F32)
         for j in range(npp)], axis=1)
    m_prev = m_s[...]
    m_new = jnp.maximum(m_prev, jnp.max(s, axis=1, keepdims=True))
    alpha = jnp.exp(m_prev - m_new)
    p = jnp.exp(s - m_new)
    l_s[...] = alpha * l_s[...] + jnp.sum(p, axis=1, keepdims=True)
    pb = p.astype(BF16)
    page = cs[0].shape[0]
    pv = jnp.dot(pb[:, 0:page], cs[0], preferred_element_type=F32)
    for j in range(1, npp):
        pv = pv + jnp.dot(pb[:, j * page:(j + 1) * page], cs[j], preferred_element_type=F32)
    acc_s[...] = alpha * acc_s[...] + pv
    m_s[...] = m_new

    @pl.when(g == pl.num_programs(1) - 1)
    def _():
        o_ref[0] = (acc_s[...] / l_s[...]).astype(BF16)


def _attn_sample(ql, qr, cn, krn, cache_c, cache_kr, page_table, *, nheads, n_new, npp=16):
    b, rows, kvl = ql.shape
    rr = qr.shape[2]
    page = cache_c.shape[1]
    n_pages = page_table.shape[1]
    assert n_pages % npp == 0 and cn.shape[1] == page
    bmap = lambda i, g, pt: (i, 0, 0)
    pmap = lambda j: (lambda i, g, pt: (pt[i, g * npp + j], 0, 0))
    grid_spec = pltpu.PrefetchScalarGridSpec(
        num_scalar_prefetch=1,
        grid=(b, n_pages // npp),
        in_specs=[pl.BlockSpec((1, rows, kvl), bmap), pl.BlockSpec((1, rows, rr), bmap),
                  pl.BlockSpec((1, page, kvl), bmap), pl.BlockSpec((1, page, rr), bmap)]
                 + [pl.BlockSpec((1, page, kvl), pmap(j)) for j in range(npp)]
                 + [pl.BlockSpec((1, page, rr), pmap(j)) for j in range(npp)],
        out_specs=pl.BlockSpec((1, rows, kvl), bmap),
        scratch_shapes=[pltpu.VMEM((rows, 1), F32), pltpu.VMEM((rows, 1), F32), pltpu.VMEM((rows, kvl), F32)],
    )
    return pl.pallas_call(
        functools.partial(_attn_sample_kernel, npp=npp, nheads=nheads, n_new=n_new),
        grid_spec=grid_spec,
        out_shape=jax.ShapeDtypeStruct((b, rows, kvl), BF16),
        compiler_params=_cparams("arbitrary", "arbitrary"),
        name="attn_sample",
    )(page_table, ql, qr, cn, krn, *([cache_c] * npp), *([cache_kr] * npp))


def _mla_out_kernel(h_ref, o_ref, wuv_ref, wo_ref, y_ref):
    npair = wuv_ref.shape[0]
    parts = [jnp.dot(jnp.concatenate([o_ref[2 * p], o_ref[2 * p + 1]], axis=-1), wuv_ref[p],
                     preferred_element_type=F32).astype(BF16) for p in range(npair)]
    y_ref[...] = h_ref[...] + jnp.dot(jnp.concatenate(parts, axis=-1), wo_ref[...], preferred_element_type=F32)


def _mla_out(h, o, wuv_bd, wo, *, tn):
    n, d = h.shape
    nheads, _, kvl = o.shape
    tok = lambda i: (i, 0)
    return pl.pallas_call(
        _mla_out_kernel,
        grid=(n // tn,),
        in_specs=[pl.BlockSpec((tn, d), tok), pl.BlockSpec((nheads, tn, kvl), lambda i: (0, i, 0)),
                  _full(wuv_bd.shape), _full(wo.shape)],
        out_specs=pl.BlockSpec((tn, d), tok),
        out_shape=jax.ShapeDtypeStruct((n, d), F32),
        compiler_params=_cparams("arbitrary"),
        name="mla_out",
    )(h, o, wuv_bd, wo)


def _rope_tables(pos, half, reps):
    inv = ROPE_THETA ** (-jnp.arange(half, dtype=F32) / half)
    ang = pos.astype(F32)[:, None] * inv[None, :]
    cos, sin = jnp.cos(ang), jnp.sin(ang)
    return (jnp.tile(jnp.concatenate([cos, cos], axis=-1), (1, reps)),
            jnp.tile(jnp.concatenate([-sin, sin], axis=-1), (1, reps)))


def _swap_halves(w):
    half = w.shape[-1] // 2
    return jnp.concatenate([w[..., half:], w[..., :half]], axis=-1)


def _pair_block_diag(w):
    h, k, n = w.shape
    z = jnp.zeros((h // 2, k, n), w.dtype)
    top = jnp.concatenate([w[0::2], z], axis=2)
    bot = jnp.concatenate([z, w[1::2]], axis=2)
    return jnp.concatenate([top, bot], axis=1)


def kernel(x_prompt, x_sample, cache_ckv, cache_krope, page_table, state_h, state_conv, ln_mix, ln_ffn, ln_final, lru_w_in, lru_conv_w, lru_conv_b, lru_gate_a_w, lru_gate_a_b, lru_gate_x_w, lru_gate_x_b, lru_lambda, lru_w_out, kv_norm, w_dkv, ckv_norm, w_kr, w_uk, w_uv, mla_w_dq, mla_q_norm, mla_w_uq, mla_w_o, peer_w_q, peer_subkeys, peer_u, peer_v):
    bp, sp, d = x_prompt.shape
    bd, sd, _ = x_sample.shape
    depth = ln_mix.shape[0]
    assert depth == 2 and lru_w_in.shape[0] == 1 and mla_w_dq.shape[0] == 1
    kvl, nheads, qk_nope = w_uk.shape
    v_head = w_uv.shape[2]
    qk_rope = w_kr.shape[1]
    page = cache_ckv.shape[1]
    past_len = page_table.shape[1] * page
    scale = float(qk_nope + qk_rope) ** -0.5
    np_tok, nd_tok = bp * sp, bd * sd

    lw = (ln_mix[0], lru_w_in[0].astype(BF16), lru_conv_w[0], lru_conv_b[0],
          lru_gate_a_w[0].astype(BF16), lru_gate_a_b[0], lru_gate_x_w[0].astype(BF16), lru_gate_x_b[0],
          lru_lambda[0], lru_w_out[0].astype(BF16))
    hp, hl_p, cb_p = _lru_prompt(x_prompt, *lw)
    hd, hl_d, cb_d = _lru_sample(x_sample, state_conv[:, 0], state_h[:, 0], *lw)
    hp = hp.reshape(np_tok, d)
    hd = hd.reshape(nd_tok, d)

    nkeys = peer_subkeys.shape[3]
    rb = 4 * nkeys
    peer_tabs = [(peer_u[l].astype(BF16),
                  jnp.swapaxes(peer_v[l].reshape(-1, rb, d), 1, 2).astype(BF16)) for l in range(depth)]

    def peer(h, l, final_norm):
        return _peer(h, ln_ffn[l], peer_w_q[l].astype(BF16), peer_subkeys[l].astype(BF16),
                     *peer_tabs[l], ln_final, final_norm=final_norm)

    hp = peer(hp, 0, False)
    hd = peer(hd, 0, False)

    pos_p = jnp.arange(sp, dtype=jnp.int32)
    pos_d = past_len + jnp.arange(sd, dtype=jnp.int32)
    kvw = (kv_norm, w_dkv.astype(BF16), ckv_norm, w_kr.astype(BF16), _swap_halves(w_kr).astype(BF16))
    tn_p, tn_d = 512, nd_tok
    cos_p, sin_p = _rope_tables(pos_p, qk_rope // 2, 1)
    cos_d, sin_d = _rope_tables(jnp.tile(pos_d, bd), qk_rope // 2, 1)
    c_p, kr_p, cb16_p, krb16_p = _shared_kv(hp, *kvw, cos_p, sin_p, tn=tn_p)
    c_d, kr_d, cb16_d, krb16_d = _shared_kv(hd, *kvw, cos_d, sin_d, tn=tn_d)

    wuq = mla_w_uq[0].reshape(-1, nheads, qk_nope + qk_rope)
    w_nope = wuq[:, :, :qk_nope].reshape(-1, nheads * qk_nope).astype(BF16)
    w_rope = wuq[:, :, qk_nope:]
    w_rope_sw = _swap_halves(w_rope).reshape(-1, nheads * qk_rope).astype(BF16)
    w_rope = w_rope.reshape(-1, nheads * qk_rope).astype(BF16)
    wuk_bd = _pair_block_diag(jnp.transpose(w_uk, (1, 2, 0))).astype(BF16)
    wuv_bd = _pair_block_diag(jnp.transpose(w_uv, (1, 0, 2))).astype(BF16)
    qw = (ln_mix[1], mla_w_dq[0].astype(BF16), mla_q_norm[0], w_nope, w_rope, w_rope_sw, wuk_bd)
    ql_p, qr_p = _mla_q(hp, *qw, jnp.tile(cos_p, (1, nheads)), jnp.tile(sin_p, (1, nheads)), scale=scale, tn=tn_p)
    ql_d, qr_d = _mla_q(hd, *qw, jnp.tile(cos_d, (1, nheads)), jnp.tile(sin_d, (1, nheads)), scale=scale, tn=tn_d)

    o_p = _attn_prompt(ql_p, qr_p, cb16_p.reshape(bp, sp, kvl), krb16_p.reshape(bp, sp, qk_rope))

    def per_batch(a):
        return jnp.transpose(a.reshape(nheads, bd, sd, -1), (1, 0, 2, 3)).reshape(bd, nheads * sd, -1)

    pad = ((0, 0), (0, page - sd), (0, 0))
    o_d = _attn_sample(per_batch(ql_d), per_batch(qr_d),
                       jnp.pad(cb16_d.reshape(bd, sd, kvl), pad), jnp.pad(krb16_d.reshape(bd, sd, qk_rope), pad),
                       cache_ckv, cache_krope, page_table, nheads=nheads, n_new=sd)
    o_d = jnp.transpose(o_d.reshape(bd, nheads, sd, kvl), (1, 0, 2, 3)).reshape(nheads, nd_tok, kvl)
    wo = mla_w_o[0].astype(BF16)
    hp = _mla_out(hp, o_p, wuv_bd, wo, tn=tn_p)
    hd = _mla_out(hd, o_d, wuv_bd, wo, tn=tn_d)

    y_p = peer(hp, 1, True)
    y_d = peer(hd, 1, True)

    return (y_p.reshape(bp, sp, d), y_d.reshape(bd, sd, d),
            hl_p, cb_p[:, None], c_p.reshape(bp, sp, kvl), kr_p.reshape(bp, sp, qk_rope),
            hl_d[:, None], cb_d[:, None], c_d.reshape(bd, sd, kvl), kr_d.reshape(bd, sd, qk_rope))
```

```python
import functools

import jax
import jax.numpy as jnp
from jax import lax
from jax.experimental import pallas as pl
from jax.experimental.pallas import tpu as pltpu

F32 = jnp.float32
BF16 = jnp.bfloat16

RMS_EPS = 1e-6
NEG_INF = -1e30
LRU_C = 8.0
CONV_W = 4
ROPE_THETA = 10000.0
PEER_TOPK = 16

LANES = 128
SUBLANES = 8
VMEM_LIMIT = 56 * 1024 * 1024

_NT = (((1,), (1,)), ((), ()))


def _cparams(*sem):
    return pltpu.CompilerParams(dimension_semantics=sem, vmem_limit_bytes=VMEM_LIMIT)


def _rms(x, g):
    return x * lax.rsqrt(jnp.mean(x * x, axis=-1, keepdims=True) + RMS_EPS) * g


def _gelu(x):
    inner = x * (0.7978845608028654 + (0.7978845608028654 * 0.044715) * (x * x))
    return x * (0.5 + 0.5 * jnp.tanh(inner))


def _sigmoid(x):
    return 1.0 / (1.0 + jnp.exp(-x))


def _softplus(x):
    return jnp.maximum(x, 0.0) + jnp.log(1.0 + jnp.exp(-jnp.abs(x)))


def _full(shape):
    n = len(shape)
    return pl.BlockSpec(shape, lambda *_: (0,) * n)


def _lru_gates(xc, gaw_ref, gab_ref, gxw_ref, gxb_ref, lam_ref):
    nh, bw, _ = gaw_ref.shape
    xcb = xc.astype(BF16)
    r_parts, i_parts = [], []
    for h in range(nh):
        blk = xcb[:, h * bw:(h + 1) * bw]
        r_parts.append(jnp.dot(blk, gaw_ref[h], preferred_element_type=F32))
        i_parts.append(jnp.dot(blk, gxw_ref[h], preferred_element_type=F32))
    r = _sigmoid(jnp.concatenate(r_parts, axis=-1) + gab_ref[...])
    i = _sigmoid(jnp.concatenate(i_parts, axis=-1) + gxb_ref[...])
    log_a = (-LRU_C) * r * _softplus(-lam_ref[...])
    a = jnp.exp(log_a)
    bx = jnp.sqrt(1.0 - jnp.exp(2.0 * log_a)) * i * xc
    return a, bx


def _lru_prompt_kernel(x_ref, g_ref, win_ref, cw_ref, cb_ref, gaw_ref, gab_ref, gxw_ref, gxb_ref,
                       lam_ref, wout_ref, y_ref, hl_ref, cbuf_ref, xcat, a_s, b_s, h_s, hcar):
    t = pl.program_id(1)
    tt, d = x_ref.shape[1], x_ref.shape[2]

    @pl.when(t == 0)
    def _():
        xcat[0:SUBLANES, :] = jnp.zeros((SUBLANES, d), F32)
        hcar[...] = jnp.zeros((1, d), F32)

    x = x_ref[0]
    xn = _rms(x, g_ref[...])
    proj = jnp.dot(xn.astype(BF16), win_ref[...], preferred_element_type=F32)
    gate = _gelu(proj[:, :d])
    xb = proj[:, d:]
    xcat[SUBLANES:SUBLANES + tt, :] = xb
    xc = (cb_ref[...]
          + cw_ref[0:1, :] * xcat[SUBLANES - 3:SUBLANES - 3 + tt, :]
          + cw_ref[1:2, :] * xcat[SUBLANES - 2:SUBLANES - 2 + tt, :]
          + cw_ref[2:3, :] * xcat[SUBLANES - 1:SUBLANES - 1 + tt, :]
          + cw_ref[3:4, :] * xb)
    tail = xcat[tt:tt + SUBLANES, :]
    xcat[0:SUBLANES, :] = tail
    cbuf_ref[0] = tail

    a, bx = _lru_gates(xc, gaw_ref, gab_ref, gxw_ref, gxb_ref, lam_ref)
    a_s[...] = a
    b_s[...] = bx

    def body(i, h):
        h = a_s[pl.ds(i, 1), :] * h + b_s[pl.ds(i, 1), :]
        h_s[pl.ds(i, 1), :] = h
        return h

    h = lax.fori_loop(0, tt, body, hcar[...], unroll=8)
    hcar[...] = h
    hl_ref[0] = h
    y = jnp.dot((h_s[...] * gate).astype(BF16), wout_ref[...], preferred_element_type=F32)
    y_ref[0] = x + y


def _lru_prompt(x, g, win, cw, cb, gaw, gab, gxw, gxb, lam, wout, *, tt=256):
    b, t, d = x.shape
    nh, bw, _ = gaw.shape
    row = lambda v: v.reshape(1, -1)
    wspecs = [_full((1, d)), _full((d, 2 * d)), _full((CONV_W, d)), _full((1, d)),
              _full((nh, bw, bw)), _full((1, d)), _full((nh, bw, bw)), _full((1, d)),
              _full((1, d)), _full((d, d))]
    y, hl, cbuf = pl.pallas_call(
        _lru_prompt_kernel,
        grid=(b, t // tt),
        in_specs=[pl.BlockSpec((1, tt, d), lambda i, j: (i, j, 0))] + wspecs,
        out_specs=[pl.BlockSpec((1, tt, d), lambda i, j: (i, j, 0)),
                   pl.BlockSpec((1, 1, d), lambda i, j: (i, 0, 0)),
                   pl.BlockSpec((1, SUBLANES, d), lambda i, j: (i, 0, 0))],
        out_shape=[jax.ShapeDtypeStruct((b, t, d), F32),
                   jax.ShapeDtypeStruct((b, 1, d), F32),
                   jax.ShapeDtypeStruct((b, SUBLANES, d), F32)],
        scratch_shapes=[pltpu.VMEM((tt + SUBLANES, d), F32), pltpu.VMEM((tt, d), F32),
                        pltpu.VMEM((tt, d), F32), pltpu.VMEM((tt, d), F32), pltpu.VMEM((1, d), F32)],
        compiler_params=_cparams("arbitrary", "arbitrary"),
        name="lru_prompt",
    )(x, row(g), win, cw, row(cb), gaw, row(gab), gxw, row(gxb), row(lam), wout)
    return y, hl, cbuf[:, SUBLANES - (CONV_W - 1):, :]


def _lru_sample_kernel(x_ref, conv_ref, h0_ref, g_ref, win_ref, cw_ref, cb_ref, gaw_ref, gab_ref,
                       gxw_ref, gxb_ref, lam_ref, wout_ref, y_ref, hl_ref, cbuf_ref):
    t, b, d = x_ref.shape
    x = x_ref[...].reshape(t * b, d)
    xn = _rms(x, g_ref[...])
    proj = jnp.dot(xn.astype(BF16), win_ref[...], preferred_element_type=F32)
    gate = _gelu(proj[:, :d])
    xb = proj[:, d:]
    seq = [conv_ref[k] for k in range(CONV_W - 1)] + [xb[k * b:(k + 1) * b, :] for k in range(t)]
    xc_parts = []
    for k in range(t):
        acc = cb_ref[...] + cw_ref[0:1, :] * seq[k]
        for j in range(1, CONV_W):
            acc = acc + cw_ref[j:j + 1, :] * seq[k + j]
        xc_parts.append(acc)
    for k in range(CONV_W - 1):
        cbuf_ref[k] = seq[t + k]
    xc = jnp.concatenate(xc_parts, axis=0)
    a, bx = _lru_gates(xc, gaw_ref, gab_ref, gxw_ref, gxb_ref, lam_ref)
    h = h0_ref[...]
    hs = []
    for k in range(t):
        h = a[k * b:(k + 1) * b, :] * h + bx[k * b:(k + 1) * b, :]
        hs.append(h)
    hl_ref[...] = h
    hall = jnp.concatenate(hs, axis=0)
    y = jnp.dot((hall * gate).astype(BF16), wout_ref[...], preferred_element_type=F32)
    y_ref[...] = (x + y).reshape(t, b, d)


def _lru_sample(x, conv, h0, g, win, cw, cb, gaw, gab, gxw, gxb, lam, wout):
    b, t, d = x.shape
    assert t >= CONV_W - 1
    row = lambda v: v.reshape(1, -1)
    y, hl, cbuf = pl.pallas_call(
        _lru_sample_kernel,
        out_shape=[jax.ShapeDtypeStruct((t, b, d), F32),
                   jax.ShapeDtypeStruct((b, d), F32),
                   jax.ShapeDtypeStruct((CONV_W - 1, b, d), F32)],
        compiler_params=pltpu.CompilerParams(vmem_limit_bytes=VMEM_LIMIT),
        name="lru_sample",
    )(jnp.swapaxes(x, 0, 1), jnp.swapaxes(conv, 0, 1), h0, row(g), win, cw, row(cb),
      gaw, row(gab), gxw, row(gxb), row(lam), wout)
    return jnp.swapaxes(y, 0, 1), hl, jnp.swapaxes(cbuf, 0, 1)


def _cand_pairs(k):
    return [(a, min(k, (k + 1) // (a + 1))) for a in range(k) if (k + 1) // (a + 1) >= 1]


def _peer_route_kernel(h_ref, g_ref, wq_ref, sk_ref, xnt_ref, rk_ref, eb_ref, cnt_ref, aw_ref,
                       q_s, st_s, sv_s, cand_s):
    tn, d = h_ref.shape
    nh, _, nk, dk2 = sk_ref.shape
    xn = _rms(h_ref[...], g_ref[...])
    xnt = xn.T.astype(BF16)
    tw = xnt_ref.shape[2]
    for k in range(xnt_ref.shape[0]):
        xnt_ref[k] = xnt[:, k * tw:(k + 1) * tw]
    q = jnp.dot(xn.astype(BF16), wq_ref[...], preferred_element_type=F32).astype(BF16)
    for hp in range(2 * nh):
        q_s[hp] = q[:, hp * dk2:(hp + 1) * dk2]

    def scores(hp, carry):
        s = lax.dot_general(sk_ref[hp // 2, hp % 2], q_s[hp], _NT, preferred_element_type=F32)
        st_s[hp] = s
        for ts in range(tn // LANES):
            cur = s[:, ts * LANES:(ts + 1) * LANES]
            for r in range(PEER_TOPK):
                m = jnp.max(cur, axis=0, keepdims=True)
                sv_s[hp, r:r + 1, ts * LANES:(ts + 1) * LANES] = m
                cur = jnp.where(cur == m, -jnp.inf, cur)
        return carry

    lax.fori_loop(0, 2 * nh, scores, 0)

    pairs = _cand_pairs(PEER_TOPK)
    ncand = sum(nb for _, nb in pairs)
    cand_s[...] = jnp.full(cand_s.shape, -jnp.inf, F32)

    def thresholds(h, carry):
        for ts in range(tn // LANES):
            sl = slice(ts * LANES, (ts + 1) * LANES)
            sv0 = sv_s[2 * h, :, sl]
            sv1 = sv_s[2 * h + 1, :, sl]
            off = 0
            for a, nb in pairs:
                cand_s[off:off + nb, :] = sv0[a:a + 1, :] + sv1[0:nb, :]
                off += nb
            cand0 = cand_s[...]
            cur = cand0
            t_hi = None
            for r in range(PEER_TOPK + 1):
                m = jnp.max(cur, axis=0, keepdims=True)
                if r == PEER_TOPK - 1:
                    t_hi = m
                cur = jnp.where(cur == m, -jnp.inf, cur)
            t_mid = 0.5 * (t_hi + m)
            m0 = sv0[0:1, :]
            m1 = sv1[0:1, :]
            z = jnp.sum(jnp.where(cand0 >= t_mid, jnp.exp(cand0 - (m0 + m1)), 0.0), axis=0, keepdims=True)
            s0 = st_s[2 * h, :, sl]
            s1 = st_s[2 * h + 1, :, sl]
            t0 = t_mid - s0
            rank = jnp.zeros(s1.shape, F32)
            count = jnp.zeros(s0.shape, F32)
            for r in range(PEER_TOPK):
                v = sv1[r:r + 1, :]
                rank = jnp.where(s1 < v, r + 1.0, rank)
                count = jnp.where(v >= t0, r + 1.0, count)
            rk_ref[h, ts] = rank.astype(BF16)
            eb_ref[h, ts] = jnp.exp(s1 - m1).astype(BF16)
            cnt_ref[h, ts] = _dup_bf16(count)
            aw_ref[h, ts] = _dup_bf16(jnp.exp(s0 - m0) * (1.0 / z))
        return carry

    assert ncand <= cand_s.shape[0]
    lax.fori_loop(0, nh, thresholds, 0)


def _row_bcast(ref, h, ts, i):
    return ref[h, ts, pl.ds(i, SUBLANES, stride=0), :]


def _dup_bf16(x):
    u = lax.bitcast_convert_type(x.astype(BF16).astype(F32), jnp.uint32)
    return u | (u >> 16)


def _row_bcast_bf16(ref, h, ts, i):
    return pltpu.bitcast(_row_bcast(ref, h, ts, i), BF16)


def _peer_dense_kernel(h_ref, xnt_ref, u_ref, vt_ref, rk_ref, eb_ref, cnt_ref, aw_ref, gf_ref, o_ref,
                       act_a, act_b, p_a, p_b, acc_s, rk_s, eb_s, *, final_norm):
    c = pl.program_id(1)
    nh, _, ic, _ = cnt_ref.shape
    nk = rk_ref.shape[2]
    ntw, _, tw = xnt_ref.shape
    nrg, _, rb = vt_ref.shape
    rg = rb // nk
    jb = min(64, nk)
    n_it = ntw * nrg

    @pl.when(c == 0)
    def _():
        acc_s[...] = jnp.zeros(acc_s.shape, F32)
        rk_s[...] = rk_ref[...]
        eb_s[...] = eb_ref[...]

    def act_dot(it):
        rows = pl.ds(pl.multiple_of((it % nrg) * rb, rb), rb)
        return jnp.dot(u_ref[rows, :], xnt_ref[it // nrg], preferred_element_type=F32)

    def out_dot(it, p_ref):
        acc_s[it // nrg] += jnp.dot(vt_ref[it % nrg], p_ref[...], preferred_element_type=F32)

    def gates(it, act_ref, p_ref, half):
        i8 = pl.ds(pl.multiple_of(((it % nrg) // 2) * SUBLANES, SUBLANES), SUBLANES)
        reps = jb // (2 * SUBLANES)

        def row_tile(tile8, row):
            x = jnp.broadcast_to(tile8[row:row + 1, :], (SUBLANES, LANES))
            return jnp.concatenate([pltpu.bitcast(x, BF16)] * reps, axis=0)

        for tsub in range(tw // LANES):
            ts = (it // nrg) * (tw // LANES) + tsub
            sl = slice(tsub * LANES, (tsub + 1) * LANES)
            cnt8 = [cnt_ref[h, ts, i8, :] for h in range(nh)]
            aw8 = [aw_ref[h, ts, i8, :] for h in range(nh)]
            for j0 in range(0, nk, jb):
                for r in range(rg):
                    g = jnp.zeros((jb, LANES), BF16)
                    for h in range(nh):
                        rk = rk_s[h, ts, j0:j0 + jb, :]
                        eb = eb_s[h, ts, j0:j0 + jb, :]
                        cnt = row_tile(cnt8[h], half * rg + r)
                        aw = row_tile(aw8[h], half * rg + r)
                        g = g + jnp.where(rk < cnt, eb, jnp.zeros_like(eb)) * aw
                    es = slice(r * nk + j0, r * nk + j0 + jb)
                    p_ref[es, sl] = _gelu(act_ref[es, sl]).astype(BF16) * g

    act_a[...] = act_dot(0)
    p_b[...] = jnp.zeros(p_b.shape, BF16)

    def item_pair(k, carry):
        it = 2 * k
        act_b[...] = act_dot(it + 1)
        gates(it, act_a, p_a, 0)
        out_dot(jnp.maximum(it - 1, 0), p_b)
        act_a[...] = act_dot(jnp.minimum(it + 2, n_it - 1))
        gates(it + 1, act_b, p_b, 1)
        out_dot(it, p_a)
        return carry

    assert nrg % 2 == 0 and 2 * rg == SUBLANES
    lax.fori_loop(0, n_it // 2, item_pair, 0)
    out_dot(n_it - 1, p_b)

    @pl.when(c == pl.num_programs(1) - 1)
    def _():
        out = h_ref[...] + jnp.concatenate([acc_s[k].T for k in range(ntw)], axis=0)
        if final_norm:
            out = _rms(out, gf_ref[...])
        o_ref[...] = out


def _peer(h, g, wq, sk, u, vt, gf, *, final_norm, tn=512, tw=256, ec=2048):
    n, d = h.shape
    nh, _, nk, dk2 = sk.shape
    ne = u.shape[0]
    rb = vt.shape[2]
    ec = min(ec, ne)
    assert ne == nk * nk and n % tn == 0 and ne % ec == 0 and ec % (nk * SUBLANES) == 0
    assert tn % tw == 0 and tw % LANES == 0 and rb % nk == 0 and ec % rb == 0
    ic = ec // nk
    ncand_pad = 56
    nts = tn // LANES
    rt_spec = pl.BlockSpec((nh, nts, nk, LANES), lambda i: (0, i, 0, 0))
    rt_shape = lambda dt: jax.ShapeDtypeStruct((nh, n // LANES, nk, LANES), dt)
    xnt, rk, eb, cnt, aw = pl.pallas_call(
        _peer_route_kernel,
        grid=(n // tn,),
        in_specs=[pl.BlockSpec((tn, d), lambda i: (i, 0)), _full((1, d)), _full(wq.shape), _full(sk.shape)],
        out_specs=[pl.BlockSpec((tn // tw, d, tw), lambda i: (i, 0, 0))] + [rt_spec] * 4,
        out_shape=[jax.ShapeDtypeStruct((n // tw, d, tw), BF16),
                   rt_shape(BF16), rt_shape(BF16), rt_shape(jnp.uint32), rt_shape(jnp.uint32)],
        scratch_shapes=[pltpu.VMEM((2 * nh, tn, dk2), BF16), pltpu.VMEM((2 * nh, nk, tn), F32),
                        pltpu.VMEM((2 * nh, PEER_TOPK, tn), F32), pltpu.VMEM((ncand_pad, LANES), F32)],
        compiler_params=_cparams("arbitrary"),
        name="peer_route",
    )(h, g.reshape(1, d), wq, sk)
    return pl.pallas_call(
        functools.partial(_peer_dense_kernel, final_norm=final_norm),
        grid=(n // tn, ne // ec),
        in_specs=[pl.BlockSpec((tn, d), lambda i, c: (i, 0)),
                  pl.BlockSpec((tn // tw, d, tw), lambda i, c: (i, 0, 0)),
                  pl.BlockSpec((ec, d), lambda i, c: (c, 0)),
                  pl.BlockSpec((ec // rb, d, rb), lambda i, c: (c, 0, 0)),
                  pl.BlockSpec((nh, nts, nk, LANES), lambda i, c: (0, i, 0, 0)),
                  pl.BlockSpec((nh, nts, nk, LANES), lambda i, c: (0, i, 0, 0)),
                  pl.BlockSpec((nh, nts, ic, LANES), lambda i, c: (0, i, c, 0)),
                  pl.BlockSpec((nh, nts, ic, LANES), lambda i, c: (0, i, c, 0)),
                  _full((1, d))],
        out_specs=pl.BlockSpec((tn, d), lambda i, c: (i, 0)),
        out_shape=jax.ShapeDtypeStruct((n, d), F32),
        scratch_shapes=[pltpu.VMEM((rb, tw), F32)] * 2 + [pltpu.VMEM((rb, tw), BF16)] * 2
                       + [pltpu.VMEM((tn // tw, d, tw), F32)] + [pltpu.VMEM((nh, nts, nk, LANES), BF16)] * 2,
        compiler_params=_cparams("arbitrary", "arbitrary"),
        name="peer_dense",
    )(h, xnt, u, vt, rk, eb, cnt, aw, gf.reshape(1, d))


def _shared_kv_kernel(h_ref, g_ref, wdkv_ref, gc_ref, wkr_ref, wkrs_ref, cos_ref, sin_ref,
                      c_ref, kr_ref, cb_ref, krb_ref):
    hn = _rms(h_ref[...], g_ref[...]).astype(BF16)
    c = _rms(jnp.dot(hn, wdkv_ref[...], preferred_element_type=F32), gc_ref[...])
    kr = (jnp.dot(hn, wkr_ref[...], preferred_element_type=F32) * cos_ref[...]
          + jnp.dot(hn, wkrs_ref[...], preferred_element_type=F32) * sin_ref[...])
    c_ref[...] = c
    kr_ref[...] = kr
    cb_ref[...] = c.astype(BF16)
    krb_ref[...] = kr.astype(BF16)


def _shared_kv(h, g, wdkv, gc, wkr, wkrs, cos, sin, *, tn):
    n, d = h.shape
    kvl, rr = wdkv.shape[1], wkr.shape[1]
    period = cos.shape[0] // tn
    tok = lambda i: (i, 0)
    return pl.pallas_call(
        _shared_kv_kernel,
        grid=(n // tn,),
        in_specs=[pl.BlockSpec((tn, d), tok), _full((1, d)), _full(wdkv.shape), _full((1, kvl)),
                  _full(wkr.shape), _full(wkrs.shape),
                  pl.BlockSpec((tn, rr), lambda i: (i % period, 0)),
                  pl.BlockSpec((tn, rr), lambda i: (i % period, 0))],
        out_specs=[pl.BlockSpec((tn, kvl), tok), pl.BlockSpec((tn, rr), tok),
                   pl.BlockSpec((tn, kvl), tok), pl.BlockSpec((tn, rr), tok)],
        out_shape=[jax.ShapeDtypeStruct((n, kvl), F32), jax.ShapeDtypeStruct((n, rr), F32),
                   jax.ShapeDtypeStruct((n, kvl), BF16), jax.ShapeDtypeStruct((n, rr), BF16)],
        compiler_params=_cparams("arbitrary"),
        name="shared_kv",
    )(h, g.reshape(1, d), wdkv, gc.reshape(1, kvl), wkr, wkrs, cos, sin)


def _mla_q_kernel(h_ref, g_ref, wdq_ref, gq_ref, wn_ref, wr_ref, wrs_ref, wuk_ref, cos_ref, sin_ref,
                  ql_ref, qr_ref, *, scale):
    xn = _rms(h_ref[...], g_ref[...]).astype(BF16)
    cq = _rms(jnp.dot(xn, wdq_ref[...], preferred_element_type=F32), gq_ref[...]).astype(BF16)
    qn = jnp.dot(cq, wn_ref[...], preferred_element_type=F32).astype(BF16)
    qr = (jnp.dot(cq, wr_ref[...], preferred_element_type=F32) * cos_ref[...]
          + jnp.dot(cq, wrs_ref[...], preferred_element_type=F32) * sin_ref[...])
    qr = (qr * scale).astype(BF16)
    nheads, _, rr = qr_ref.shape
    for h in range(nheads):
        qr_ref[h] = qr[:, h * rr:(h + 1) * rr]
    npair, pw, ow = wuk_ref.shape
    for p in range(npair):
        ql = jnp.dot(qn[:, p * pw:(p + 1) * pw], wuk_ref[p], preferred_element_type=F32)
        ql = (ql * scale).astype(BF16)
        ql_ref[2 * p] = ql[:, :ow // 2]
        ql_ref[2 * p + 1] = ql[:, ow // 2:]


def _mla_q(h, g, wdq, gq, wn, wr, wrs, wuk_bd, cos, sin, *, scale, tn):
    n, d = h.shape
    nheads = 2 * wuk_bd.shape[0]
    kvl = wuk_bd.shape[2] // 2
    qr_w = wr.shape[1]
    rr = qr_w // nheads
    period = cos.shape[0] // tn
    tok = lambda i: (i, 0)
    htok = lambda i: (0, i, 0)
    return pl.pallas_call(
        functools.partial(_mla_q_kernel, scale=scale),
        grid=(n // tn,),
        in_specs=[pl.BlockSpec((tn, d), tok), _full((1, d)), _full(wdq.shape), _full((1, wdq.shape[1])),
                  _full(wn.shape), _full(wr.shape), _full(wrs.shape), _full(wuk_bd.shape),
                  pl.BlockSpec((tn, qr_w), lambda i: (i % period, 0)),
                  pl.BlockSpec((tn, qr_w), lambda i: (i % period, 0))],
        out_specs=[pl.BlockSpec((nheads, tn, kvl), htok), pl.BlockSpec((nheads, tn, rr), htok)],
        out_shape=[jax.ShapeDtypeStruct((nheads, n, kvl), BF16), jax.ShapeDtypeStruct((nheads, n, rr), BF16)],
        compiler_params=_cparams("arbitrary"),
        name="mla_q",
    )(h, g.reshape(1, d), wdq, gq.reshape(1, -1), wn, wr, wrs, wuk_bd, cos, sin)


def _softmax_step(s, c, m_s, l_s, acc_s, rows=slice(None)):
    m_prev = m_s[rows]
    m_new = jnp.maximum(m_prev, jnp.max(s, axis=1, keepdims=True))
    alpha = jnp.exp(m_prev - m_new)
    p = jnp.exp(s - m_new)
    l_s[rows] = alpha * l_s[rows] + jnp.sum(p, axis=1, keepdims=True)
    acc_s[rows] = alpha * acc_s[rows] + jnp.dot(p.astype(BF16), c, preferred_element_type=F32)
    m_s[rows] = m_new


def _attn_prompt_kernel(ql_ref, qr_ref, c_ref, kr_ref, o_ref, m_s, l_s, acc_s, *, tq, tk, hc):
    qi, ki = pl.program_id(1), pl.program_id(2)
    nheads, _, kvl = ql_ref.shape
    rr = qr_ref.shape[2]
    q_lo, k_lo = qi * tq, ki * tk

    @pl.when(ki == 0)
    def _():
        m_s[...] = jnp.full(m_s.shape, NEG_INF, F32)
        l_s[...] = jnp.zeros(l_s.shape, F32)
        acc_s[...] = jnp.zeros(acc_s.shape, F32)

    def step(masked):
        c, kr = c_ref[0], kr_ref[0]
        rows = hc * tq
        for h0 in range(0, nheads, hc):
            ql = ql_ref[h0:h0 + hc].reshape(rows, kvl)
            qr = qr_ref[h0:h0 + hc].reshape(rows, rr)
            s = (lax.dot_general(ql, c, _NT, preferred_element_type=F32)
                 + lax.dot_general(qr, kr, _NT, preferred_element_type=F32))
            if masked:
                qpos = q_lo + (lax.broadcasted_iota(jnp.int32, (rows, tk), 0) & (tq - 1))
                kpos = k_lo + lax.broadcasted_iota(jnp.int32, (rows, tk), 1)
                s = jnp.where(kpos <= qpos, s, NEG_INF)
            _softmax_step(s, c, m_s, l_s, acc_s, slice(h0 * tq, (h0 + hc) * tq))

    @pl.when(k_lo + tk - 1 <= q_lo)
    def _():
        step(False)

    @pl.when((k_lo + tk - 1 > q_lo) & (k_lo <= q_lo + tq - 1))
    def _():
        step(True)

    @pl.when(ki == pl.num_programs(2) - 1)
    def _():
        o_ref[...] = (acc_s[...] / l_s[...]).astype(BF16).reshape(nheads, tq, kvl)


def _attn_prompt(ql, qr, cb, krb, *, tq=128, tk=512, hc=4):
    nheads, n, kvl = ql.shape
    rr = qr.shape[2]
    b, t, _ = cb.shape
    assert tq & (tq - 1) == 0 and t % tq == 0 and t % tk == 0 and nheads % hc == 0
    nq = t // tq
    rows = tq * nheads
    qmap = lambda i, q, k: (0, i * nq + q, 0)
    kmap = lambda i, q, k: (i, jnp.minimum(k, (q * tq + tq - 1) // tk), 0)
    return pl.pallas_call(
        functools.partial(_attn_prompt_kernel, tq=tq, tk=tk, hc=hc),
        grid=(b, nq, t // tk),
        in_specs=[pl.BlockSpec((nheads, tq, kvl), qmap), pl.BlockSpec((nheads, tq, rr), qmap),
                  pl.BlockSpec((1, tk, kvl), kmap), pl.BlockSpec((1, tk, rr), kmap)],
        out_specs=pl.BlockSpec((nheads, tq, kvl), qmap),
        out_shape=jax.ShapeDtypeStruct((nheads, n, kvl), BF16),
        scratch_shapes=[pltpu.VMEM((rows, 1), F32), pltpu.VMEM((rows, 1), F32), pltpu.VMEM((rows, kvl), F32)],
        compiler_params=_cparams("arbitrary", "arbitrary", "arbitrary"),
        name="attn_prompt",
    )(ql, qr, cb, krb)


def _attn_sample_kernel(pt_ref, ql_ref, qr_ref, cn_ref, krn_ref, *rest, npp, nheads, n_new):
    del pt_ref
    cpages, krpages = rest[:npp], rest[npp:2 * npp]
    o_ref, m_s, l_s, acc_s = rest[2 * npp:]
    g = pl.program_id(1)
    ql, qr = ql_ref[0], qr_ref[0]
    rows = ql.shape[0]

    @pl.when(g == 0)
    def _():
        m_s[...] = jnp.full(m_s.shape, NEG_INF, F32)
        l_s[...] = jnp.zeros(l_s.shape, F32)
        acc_s[...] = jnp.zeros(acc_s.shape, F32)
        c = cn_ref[0]
        s = (lax.dot_general(ql, c, _NT, preferred_element_type=F32)
             + lax.dot_general(qr, krn_ref[0], _NT, preferred_element_type=F32))
        qpos = lax.rem(lax.broadcasted_iota(jnp.int32, s.shape, 0), n_new)
        kpos = lax.broadcasted_iota(jnp.int32, s.shape, 1)
        s = jnp.where((kpos <= qpos) & (kpos < n_new), s, NEG_INF)
        _softmax_step(s, c, m_s, l_s, acc_s)

    c = jnp.concatenate([r[0].astype(BF16) for r in cpages], axis=0)
    krt = jnp.concatenate([r[0].astype(BF16) for r in krpages], axis=1)
    s = (lax.dot_general(ql, c, _NT, preferred_element_type=F32)
         + jnp.dot(qr, krt, preferred_element_type=F32))
    _softmax_step(s, c, m_s, l_s, acc_s)

    @pl.when(g == pl.num_programs(1) - 1)
    def _():
        o_ref[0] = (acc_s[...] / l_s[...]).astype(BF16)


def _attn_sample(ql, qr, cn, krn, cache_c, cache_krt, page_table, *, nheads, n_new, npp=32):
    b, rows, kvl = ql.shape
    rr = qr.shape[2]
    page = cache_c.shape[1]
    n_pages = page_table.shape[1]
    assert n_pages % npp == 0 and cn.shape[1] == page
    bmap = lambda i, g, pt: (i, 0, 0)
    pmap = lambda j: (lambda i, g, pt: (pt[i, g * npp + j], 0, 0))
    grid_spec = pltpu.PrefetchScalarGridSpec(
        num_scalar_prefetch=1,
        grid=(b, n_pages // npp),
        in_specs=[pl.BlockSpec((1, rows, kvl), bmap), pl.BlockSpec((1, rows, rr), bmap),
                  pl.BlockSpec((1, page, kvl), bmap), pl.BlockSpec((1, page, rr), bmap)]
                 + [pl.BlockSpec((1, page, kvl), pmap(j)) for j in range(npp)]
                 + [pl.BlockSpec((1, rr, page), pmap(j)) for j in range(npp)],
        out_specs=pl.BlockSpec((1, rows, kvl), bmap),
        scratch_shapes=[pltpu.VMEM((rows, 1), F32), pltpu.VMEM((rows, 1), F32), pltpu.VMEM((rows, kvl), F32)],
    )
    return pl.pallas_call(
        functools.partial(_attn_sample_kernel, npp=npp, nheads=nheads, n_new=n_new),
        grid_spec=grid_spec,
        out_shape=jax.ShapeDtypeStruct((b, rows, kvl), BF16),
        compiler_params=_cparams("arbitrary", "arbitrary"),
        name="attn_sample",
    )(page_table, ql, qr, cn, krn, *([cache_c] * npp), *([cache_krt] * npp))


def _mla_out_kernel(h_ref, o_ref, wuv_ref, wo_ref, y_ref):
    npair = wuv_ref.shape[0]
    parts = [jnp.dot(jnp.concatenate([o_ref[2 * p], o_ref[2 * p + 1]], axis=-1), wuv_ref[p],
                     preferred_element_type=F32).astype(BF16) for p in range(npair)]
    y_ref[...] = h_ref[...] + jnp.dot(jnp.concatenate(parts, axis=-1), wo_ref[...], preferred_element_type=F32)


def _mla_out(h, o, wuv_bd, wo, *, tn):
    n, d = h.shape
    nheads, _, kvl = o.shape
    tok = lambda i: (i, 0)
    return pl.pallas_call(
        _mla_out_kernel,
        grid=(n // tn,),
        in_specs=[pl.BlockSpec((tn, d), tok), pl.BlockSpec((nheads, tn, kvl), lambda i: (0, i, 0)),
                  _full(wuv_bd.shape), _full(wo.shape)],
        out_specs=pl.BlockSpec((tn, d), tok),
        out_shape=jax.ShapeDtypeStruct((n, d), F32),
        compiler_params=_cparams("arbitrary"),
        name="mla_out",
    )(h, o, wuv_bd, wo)


def _rope_tables(pos, half, reps):
    inv = ROPE_THETA ** (-jnp.arange(half, dtype=F32) / half)
    ang = pos.astype(F32)[:, None] * inv[None, :]
    cos, sin = jnp.cos(ang), jnp.sin(ang)
    return (jnp.tile(jnp.concatenate([cos, cos], axis=-1), (1, reps)),
            jnp.tile(jnp.concatenate([-sin, sin], axis=-1), (1, reps)))


def _swap_halves(w):
    half = w.shape[-1] // 2
    return jnp.concatenate([w[..., half:], w[..., :half]], axis=-1)


def _pair_block_diag(w):
    h, k, n = w.shape
    z = jnp.zeros((h // 2, k, n), w.dtype)
    top = jnp.concatenate([w[0::2], z], axis=2)
    bot = jnp.concatenate([z, w[1::2]], axis=2)
    return jnp.concatenate([top, bot], axis=1)


def kernel(x_prompt, x_sample, cache_ckv, cache_krope, page_table, state_h, state_conv, ln_mix, ln_ffn, ln_final, lru_w_in, lru_conv_w, lru_conv_b, lru_gate_a_w, lru_gate_a_b, lru_gate_x_w, lru_gate_x_b, lru_lambda, lru_w_out, kv_norm, w_dkv, ckv_norm, w_kr, w_uk, w_uv, mla_w_dq, mla_q_norm, mla_w_uq, mla_w_o, peer_w_q, peer_subkeys, peer_u, peer_v):
    bp, sp, d = x_prompt.shape
    bd, sd, _ = x_sample.shape
    depth = ln_mix.shape[0]
    assert depth == 2 and lru_w_in.shape[0] == 1 and mla_w_dq.shape[0] == 1
    kvl, nheads, qk_nope = w_uk.shape
    v_head = w_uv.shape[2]
    qk_rope = w_kr.shape[1]
    page = cache_ckv.shape[1]
    past_len = page_table.shape[1] * page
    scale = float(qk_nope + qk_rope) ** -0.5
    np_tok, nd_tok = bp * sp, bd * sd

    lw = (ln_mix[0], lru_w_in[0].astype(BF16), lru_conv_w[0], lru_conv_b[0],
          lru_gate_a_w[0].astype(BF16), lru_gate_a_b[0], lru_gate_x_w[0].astype(BF16), lru_gate_x_b[0],
          lru_lambda[0], lru_w_out[0].astype(BF16))
    hp, hl_p, cb_p = _lru_prompt(x_prompt, *lw)
    hd, hl_d, cb_d = _lru_sample(x_sample, state_conv[:, 0], state_h[:, 0], *lw)
    hp = hp.reshape(np_tok, d)
    hd = hd.reshape(nd_tok, d)

    nkeys = peer_subkeys.shape[3]
    rb = 4 * nkeys
    peer_tabs = [(peer_u[l].astype(BF16),
                  jnp.swapaxes(peer_v[l].astype(BF16).reshape(-1, rb, d), 1, 2)) for l in range(depth)]

    def peer(h, l, final_norm):
        return _peer(h, ln_ffn[l], peer_w_q[l].astype(BF16), peer_subkeys[l].astype(BF16),
                     *peer_tabs[l], ln_final, final_norm=final_norm)

    hp = peer(hp, 0, False)
    hd = peer(hd, 0, False)

    pos_p = jnp.arange(sp, dtype=jnp.int32)
    pos_d = past_len + jnp.arange(sd, dtype=jnp.int32)
    kvw = (kv_norm, w_dkv.astype(BF16), ckv_norm, w_kr.astype(BF16), _swap_halves(w_kr).astype(BF16))
    tn_p, tn_d = 512, nd_tok
    cos_p, sin_p = _rope_tables(pos_p, qk_rope // 2, 1)
    cos_d, sin_d = _rope_tables(jnp.tile(pos_d, bd), qk_rope // 2, 1)
    c_p, kr_p, cb16_p, krb16_p = _shared_kv(hp, *kvw, cos_p, sin_p, tn=tn_p)
    c_d, kr_d, cb16_d, krb16_d = _shared_kv(hd, *kvw, cos_d, sin_d, tn=tn_d)

    wuq = mla_w_uq[0].reshape(-1, nheads, qk_nope + qk_rope)
    w_nope = wuq[:, :, :qk_nope].reshape(-1, nheads * qk_nope).astype(BF16)
    w_rope = wuq[:, :, qk_nope:]
    w_rope_sw = _swap_halves(w_rope).reshape(-1, nheads * qk_rope).astype(BF16)
    w_rope = w_rope.reshape(-1, nheads * qk_rope).astype(BF16)
    wuk_bd = _pair_block_diag(jnp.transpose(w_uk, (1, 2, 0))).astype(BF16)
    wuv_bd = _pair_block_diag(jnp.transpose(w_uv, (1, 0, 2))).astype(BF16)
    qw = (ln_mix[1], mla_w_dq[0].astype(BF16), mla_q_norm[0], w_nope, w_rope, w_rope_sw, wuk_bd)
    ql_p, qr_p = _mla_q(hp, *qw, jnp.tile(cos_p, (1, nheads)), jnp.tile(sin_p, (1, nheads)), scale=scale, tn=tn_p)
    ql_d, qr_d = _mla_q(hd, *qw, jnp.tile(cos_d, (1, nheads)), jnp.tile(sin_d, (1, nheads)), scale=scale, tn=tn_d)

    o_p = _attn_prompt(ql_p, qr_p, cb16_p.reshape(bp, sp, kvl), krb16_p.reshape(bp, sp, qk_rope))

    def per_batch(a):
        return jnp.transpose(a.reshape(nheads, bd, sd, -1), (1, 0, 2, 3)).reshape(bd, nheads * sd, -1)

    pad = ((0, 0), (0, page - sd), (0, 0))
    o_d = _attn_sample(per_batch(ql_d), per_batch(qr_d),
                       jnp.pad(cb16_d.reshape(bd, sd, kvl), pad), jnp.pad(krb16_d.reshape(bd, sd, qk_rope), pad),
                       cache_ckv, jnp.swapaxes(cache_krope, 1, 2), page_table, nheads=nheads, n_new=sd)
    o_d = jnp.transpose(o_d.reshape(bd, nheads, sd, kvl), (1, 0, 2, 3)).reshape(nheads, nd_tok, kvl)
    wo = mla_w_o[0].astype(BF16)
    hp = _mla_out(hp, o_p, wuv_bd, wo, tn=tn_p)
    hd = _mla_out(hd, o_d, wuv_bd, wo, tn=tn_d)

    y_p = peer(hp, 1, True)
    y_d = peer(hd, 1, True)

    return (y_p.reshape(bp, sp, d), y_d.reshape(bd, sd, d),
            hl_p, cb_p[:, None], c_p.reshape(bp, sp, kvl), kr_p.reshape(bp, sp, qk_rope),
            hl_d[:, None], cb_d[:, None], c_d.reshape(bd, sd, kvl), kr_d.reshape(bd, sd, qk_rope))
```

```python
import functools

import jax
import jax.numpy as jnp
from jax import lax
from jax.experimental import pallas as pl
from jax.experimental.pallas import tpu as pltpu

F32 = jnp.float32
BF16 = jnp.bfloat16

RMS_EPS = 1e-6
NEG_INF = -1e30
LRU_C = 8.0
CONV_W = 4
ROPE_THETA = 10000.0
PEER_TOPK = 16

LANES = 128
SUBLANES = 8
VMEM_LIMIT = 56 * 1024 * 1024

_NT = (((1,), (1,)), ((), ()))


def _cparams(*sem):
    return pltpu.CompilerParams(dimension_semantics=sem, vmem_limit_bytes=VMEM_LIMIT)


def _rms(x, g):
    return x * lax.rsqrt(jnp.mean(x * x, axis=-1, keepdims=True) + RMS_EPS) * g


def _gelu(x):
    inner = x * (0.7978845608028654 + (0.7978845608028654 * 0.044715) * (x * x))
    return x * (0.5 + 0.5 * jnp.tanh(inner))


def _sigmoid(x):
    return 1.0 / (1.0 + jnp.exp(-x))


def _softplus(x):
    return jnp.maximum(x, 0.0) + jnp.log(1.0 + jnp.exp(-jnp.abs(x)))


def _full(shape):
    n = len(shape)
    return pl.BlockSpec(shape, lambda *_: (0,) * n)


def _lru_gates(xc, gaw_ref, gab_ref, gxw_ref, gxb_ref, lam_ref):
    nh, bw, _ = gaw_ref.shape
    xcb = xc.astype(BF16)
    r_parts, i_parts = [], []
    for h in range(nh):
        blk = xcb[:, h * bw:(h + 1) * bw]
        r_parts.append(jnp.dot(blk, gaw_ref[h], preferred_element_type=F32))
        i_parts.append(jnp.dot(blk, gxw_ref[h], preferred_element_type=F32))
    r = _sigmoid(jnp.concatenate(r_parts, axis=-1) + gab_ref[...])
    i = _sigmoid(jnp.concatenate(i_parts, axis=-1) + gxb_ref[...])
    log_a = (-LRU_C) * r * _softplus(-lam_ref[...])
    a = jnp.exp(log_a)
    bx = jnp.sqrt(1.0 - jnp.exp(2.0 * log_a)) * i * xc
    return a, bx


def _lru_prompt_kernel(x_ref, g_ref, win_ref, cw_ref, cb_ref, gaw_ref, gab_ref, gxw_ref, gxb_ref,
                       lam_ref, wout_ref, y_ref, hl_ref, cbuf_ref, xcat, a_s, b_s, h_s, hcar):
    t = pl.program_id(1)
    tt, d = x_ref.shape[1], x_ref.shape[2]

    @pl.when(t == 0)
    def _():
        xcat[0:SUBLANES, :] = jnp.zeros((SUBLANES, d), F32)
        hcar[...] = jnp.zeros((1, d), F32)

    x = x_ref[0]
    xn = _rms(x, g_ref[...])
    proj = jnp.dot(xn.astype(BF16), win_ref[...], preferred_element_type=F32)
    gate = _gelu(proj[:, :d])
    xb = proj[:, d:]
    xcat[SUBLANES:SUBLANES + tt, :] = xb
    xc = (cb_ref[...]
          + cw_ref[0:1, :] * xcat[SUBLANES - 3:SUBLANES - 3 + tt, :]
          + cw_ref[1:2, :] * xcat[SUBLANES - 2:SUBLANES - 2 + tt, :]
          + cw_ref[2:3, :] * xcat[SUBLANES - 1:SUBLANES - 1 + tt, :]
          + cw_ref[3:4, :] * xb)
    tail = xcat[tt:tt + SUBLANES, :]
    xcat[0:SUBLANES, :] = tail
    cbuf_ref[0] = tail

    a, bx = _lru_gates(xc, gaw_ref, gab_ref, gxw_ref, gxb_ref, lam_ref)
    a_s[...] = a
    b_s[...] = bx

    def body(i, h):
        h = a_s[pl.ds(i, 1), :] * h + b_s[pl.ds(i, 1), :]
        h_s[pl.ds(i, 1), :] = h
        return h

    h = lax.fori_loop(0, tt, body, hcar[...], unroll=8)
    hcar[...] = h
    hl_ref[0] = h
    y = jnp.dot((h_s[...] * gate).astype(BF16), wout_ref[...], preferred_element_type=F32)
    y_ref[0] = x + y


def _lru_prompt(x, g, win, cw, cb, gaw, gab, gxw, gxb, lam, wout, *, tt=256):
    b, t, d = x.shape
    nh, bw, _ = gaw.shape
    row = lambda v: v.reshape(1, -1)
    wspecs = [_full((1, d)), _full((d, 2 * d)), _full((CONV_W, d)), _full((1, d)),
              _full((nh, bw, bw)), _full((1, d)), _full((nh, bw, bw)), _full((1, d)),
              _full((1, d)), _full((d, d))]
    y, hl, cbuf = pl.pallas_call(
        _lru_prompt_kernel,
        grid=(b, t // tt),
        in_specs=[pl.BlockSpec((1, tt, d), lambda i, j: (i, j, 0))] + wspecs,
        out_specs=[pl.BlockSpec((1, tt, d), lambda i, j: (i, j, 0)),
                   pl.BlockSpec((1, 1, d), lambda i, j: (i, 0, 0)),
                   pl.BlockSpec((1, SUBLANES, d), lambda i, j: (i, 0, 0))],
        out_shape=[jax.ShapeDtypeStruct((b, t, d), F32),
                   jax.ShapeDtypeStruct((b, 1, d), F32),
                   jax.ShapeDtypeStruct((b, SUBLANES, d), F32)],
        scratch_shapes=[pltpu.VMEM((tt + SUBLANES, d), F32), pltpu.VMEM((tt, d), F32),
                        pltpu.VMEM((tt, d), F32), pltpu.VMEM((tt, d), F32), pltpu.VMEM((1, d), F32)],
        compiler_params=_cparams("arbitrary", "arbitrary"),
        name="lru_prompt",
    )(x, row(g), win, cw, row(cb), gaw, row(gab), gxw, row(gxb), row(lam), wout)
    return y, hl, cbuf[:, SUBLANES - (CONV_W - 1):, :]


def _lru_sample_kernel(x_ref, conv_ref, h0_ref, g_ref, win_ref, cw_ref, cb_ref, gaw_ref, gab_ref,
                       gxw_ref, gxb_ref, lam_ref, wout_ref, y_ref, hl_ref, cbuf_ref):
    t, b, d = x_ref.shape
    x = x_ref[...].reshape(t * b, d)
    xn = _rms(x, g_ref[...])
    proj = jnp.dot(xn.astype(BF16), win_ref[...], preferred_element_type=F32)
    gate = _gelu(proj[:, :d])
    xb = proj[:, d:]
    seq = [conv_ref[k] for k in range(CONV_W - 1)] + [xb[k * b:(k + 1) * b, :] for k in range(t)]
    xc_parts = []
    for k in range(t):
        acc = cb_ref[...] + cw_ref[0:1, :] * seq[k]
        for j in range(1, CONV_W):
            acc = acc + cw_ref[j:j + 1, :] * seq[k + j]
        xc_parts.append(acc)
    for k in range(CONV_W - 1):
        cbuf_ref[k] = seq[t + k]
    xc = jnp.concatenate(xc_parts, axis=0)
    a, bx = _lru_gates(xc, gaw_ref, gab_ref, gxw_ref, gxb_ref, lam_ref)
    h = h0_ref[...]
    hs = []
    for k in range(t):
        h = a[k * b:(k + 1) * b, :] * h + bx[k * b:(k + 1) * b, :]
        hs.append(h)
    hl_ref[...] = h
    hall = jnp.concatenate(hs, axis=0)
    y = jnp.dot((hall * gate).astype(BF16), wout_ref[...], preferred_element_type=F32)
    y_ref[...] = (x + y).reshape(t, b, d)


def _lru_sample(x, conv, h0, g, win, cw, cb, gaw, gab, gxw, gxb, lam, wout):
    b, t, d = x.shape
    assert t >= CONV_W - 1
    row = lambda v: v.reshape(1, -1)
    y, hl, cbuf = pl.pallas_call(
        _lru_sample_kernel,
        out_shape=[jax.ShapeDtypeStruct((t, b, d), F32),
                   jax.ShapeDtypeStruct((b, d), F32),
                   jax.ShapeDtypeStruct((CONV_W - 1, b, d), F32)],
        compiler_params=pltpu.CompilerParams(vmem_limit_bytes=VMEM_LIMIT),
        name="lru_sample",
    )(jnp.swapaxes(x, 0, 1), jnp.swapaxes(conv, 0, 1), h0, row(g), win, cw, row(cb),
      gaw, row(gab), gxw, row(gxb), row(lam), wout)
    return jnp.swapaxes(y, 0, 1), hl, jnp.swapaxes(cbuf, 0, 1)


def _cand_pairs(k):
    return [(a, min(k, (k + 1) // (a + 1))) for a in range(k) if (k + 1) // (a + 1) >= 1]


def _peer_route_kernel(h_ref, g_ref, wq_ref, sk_ref, xnt_ref, rk_ref, eb_ref, cnt_ref, aw_ref,
                       q_s, st_s, sv_s, cand_s):
    tn, d = h_ref.shape
    nh, _, nk, dk2 = sk_ref.shape
    xn = _rms(h_ref[...], g_ref[...])
    xnt = xn.T.astype(BF16)
    tw = xnt_ref.shape[2]
    for k in range(xnt_ref.shape[0]):
        xnt_ref[k] = xnt[:, k * tw:(k + 1) * tw]
    q = jnp.dot(xn.astype(BF16), wq_ref[...], preferred_element_type=F32).astype(BF16)
    for hp in range(2 * nh):
        q_s[hp] = q[:, hp * dk2:(hp + 1) * dk2]

    def scores(hp, carry):
        s = lax.dot_general(sk_ref[hp // 2, hp % 2], q_s[hp], _NT, preferred_element_type=F32)
        st_s[hp] = s
        for ts in range(tn // LANES):
            cur = s[:, ts * LANES:(ts + 1) * LANES]
            for r in range(PEER_TOPK):
                m = jnp.max(cur, axis=0, keepdims=True)
                sv_s[hp, r:r + 1, ts * LANES:(ts + 1) * LANES] = m
                cur = jnp.where(cur == m, -jnp.inf, cur)
        return carry

    lax.fori_loop(0, 2 * nh, scores, 0)

    pairs = _cand_pairs(PEER_TOPK)
    ncand = sum(nb for _, nb in pairs)
    cand_s[...] = jnp.full(cand_s.shape, -jnp.inf, F32)

    def thresholds(h, carry):
        for ts in range(tn // LANES):
            sl = slice(ts * LANES, (ts + 1) * LANES)
            sv0 = sv_s[2 * h, :, sl]
            sv1 = sv_s[2 * h + 1, :, sl]
            off = 0
            for a, nb in pairs:
                cand_s[off:off + nb, :] = sv0[a:a + 1, :] + sv1[0:nb, :]
                off += nb
            cand0 = cand_s[...]
            cur = cand0
            t_hi = None
            for r in range(PEER_TOPK + 1):
                m = jnp.max(cur, axis=0, keepdims=True)
                if r == PEER_TOPK - 1:
                    t_hi = m
                cur = jnp.where(cur == m, -jnp.inf, cur)
            t_mid = 0.5 * (t_hi + m)
            m0 = sv0[0:1, :]
            m1 = sv1[0:1, :]
            z = jnp.sum(jnp.where(cand0 >= t_mid, jnp.exp(cand0 - (m0 + m1)), 0.0), axis=0, keepdims=True)
            s0 = st_s[2 * h, :, sl]
            s1 = st_s[2 * h + 1, :, sl]
            t0 = t_mid - s0
            rank = jnp.zeros(s1.shape, F32)
            count = jnp.zeros(s0.shape, F32)
            for r in range(PEER_TOPK):
                v = sv1[r:r + 1, :]
                rank = jnp.where(s1 < v, r + 1.0, rank)
                count = jnp.where(v >= t0, r + 1.0, count)
            rk_ref[h, ts] = rank.astype(BF16)
            eb_ref[h, ts] = jnp.exp(s1 - m1).astype(BF16)
            cnt_ref[h, ts] = _dup_bf16(count)
            aw_ref[h, ts] = _dup_bf16(jnp.exp(s0 - m0) * (1.0 / z))
        return carry

    assert ncand <= cand_s.shape[0]
    lax.fori_loop(0, nh, thresholds, 0)


def _row_bcast(ref, h, ts, i):
    return ref[h, ts, pl.ds(i, SUBLANES, stride=0), :]


def _dup_bf16(x):
    u = lax.bitcast_convert_type(x.astype(BF16).astype(F32), jnp.uint32)
    return u | (u >> 16)


def _row_bcast_bf16(ref, h, ts, i):
    return pltpu.bitcast(_row_bcast(ref, h, ts, i), BF16)


def _peer_dense_kernel(h_ref, xnt_ref, u_ref, vt_ref, rk_ref, eb_ref, cnt_ref, aw_ref, gf_ref, o_ref,
                       act_a, act_b, p_a, p_b, acc_s, rk_s, eb_s, *, final_norm):
    c = pl.program_id(1)
    nh, _, ic, _ = cnt_ref.shape
    nk = rk_ref.shape[2]
    ntw, _, tw = xnt_ref.shape
    nrg, _, rb = vt_ref.shape
    rg = rb // nk
    jb = min(64, nk)
    n_it = ntw * nrg

    @pl.when(c == 0)
    def _():
        acc_s[...] = jnp.zeros(acc_s.shape, F32)
        rk_s[...] = rk_ref[...]
        eb_s[...] = eb_ref[...]

    def act_dot(it):
        rows = pl.ds(pl.multiple_of((it % nrg) * rb, rb), rb)
        return jnp.dot(u_ref[rows, :], xnt_ref[it // nrg], preferred_element_type=F32)

    def out_dot(it, p_ref):
        acc_s[it // nrg] += jnp.dot(vt_ref[it % nrg], p_ref[...], preferred_element_type=F32)

    def gates(it, act_ref, p_ref, half):
        i8 = pl.ds(pl.multiple_of(((it % nrg) // 2) * SUBLANES, SUBLANES), SUBLANES)
        reps = jb // (2 * SUBLANES)

        def row_tile(tile8, row):
            x = jnp.broadcast_to(tile8[row:row + 1, :], (SUBLANES, LANES))
            return jnp.concatenate([pltpu.bitcast(x, BF16)] * reps, axis=0)

        for tsub in range(tw // LANES):
            ts = (it // nrg) * (tw // LANES) + tsub
            sl = slice(tsub * LANES, (tsub + 1) * LANES)
            cnt8 = [cnt_ref[h, ts, i8, :] for h in range(nh)]
            aw8 = [aw_ref[h, ts, i8, :] for h in range(nh)]
            for j0 in range(0, nk, jb):
                for r in range(rg):
                    g = jnp.zeros((jb, LANES), BF16)
                    for h in range(nh):
                        rk = rk_s[h, ts, j0:j0 + jb, :]
                        eb = eb_s[h, ts, j0:j0 + jb, :]
                        cnt = row_tile(cnt8[h], half * rg + r)
                        aw = row_tile(aw8[h], half * rg + r)
                        g = g + jnp.where(rk < cnt, eb, jnp.zeros_like(eb)) * aw
                    es = slice(r * nk + j0, r * nk + j0 + jb)
                    p_ref[es, sl] = _gelu(act_ref[es, sl]).astype(BF16) * g

    act_a[...] = act_dot(0)
    p_b[...] = jnp.zeros(p_b.shape, BF16)

    def item_pair(k, carry):
        it = 2 * k
        act_b[...] = act_dot(it + 1)
        gates(it, act_a, p_a, 0)
        out_dot(jnp.maximum(it - 1, 0), p_b)
        act_a[...] = act_dot(jnp.minimum(it + 2, n_it - 1))
        gates(it + 1, act_b, p_b, 1)
        out_dot(it, p_a)
        return carry

    assert nrg % 2 == 0 and 2 * rg == SUBLANES
    lax.fori_loop(0, n_it // 2, item_pair, 0)
    out_dot(n_it - 1, p_b)

    @pl.when(c == pl.num_programs(1) - 1)
    def _():
        out = h_ref[...] + jnp.concatenate([acc_s[k].T for k in range(ntw)], axis=0)
        if final_norm:
            out = _rms(out, gf_ref[...])
        o_ref[...] = out


def _peer(h, g, wq, sk, u, vt, gf, *, final_norm, tn=512, td=1024, tw=256, ec=1024):
    n, d = h.shape
    nh, _, nk, dk2 = sk.shape
    ne = u.shape[0]
    rb = vt.shape[2]
    ec = min(ec, ne)
    assert ne == nk * nk and n % tn == 0 and ne % ec == 0 and ec % (nk * SUBLANES) == 0
    assert tn % tw == 0 and tw % LANES == 0 and rb % nk == 0 and ec % rb == 0
    ic = ec // nk
    ncand_pad = 56
    nts = tn // LANES
    rt_spec = pl.BlockSpec((nh, nts, nk, LANES), lambda i: (0, i, 0, 0))
    rt_shape = lambda dt: jax.ShapeDtypeStruct((nh, n // LANES, nk, LANES), dt)
    xnt, rk, eb, cnt, aw = pl.pallas_call(
        _peer_route_kernel,
        grid=(n // tn,),
        in_specs=[pl.BlockSpec((tn, d), lambda i: (i, 0)), _full((1, d)), _full(wq.shape), _full(sk.shape)],
        out_specs=[pl.BlockSpec((tn // tw, d, tw), lambda i: (i, 0, 0))] + [rt_spec] * 4,
        out_shape=[jax.ShapeDtypeStruct((n // tw, d, tw), BF16),
                   rt_shape(BF16), rt_shape(BF16), rt_shape(jnp.uint32), rt_shape(jnp.uint32)],
        scratch_shapes=[pltpu.VMEM((2 * nh, tn, dk2), BF16), pltpu.VMEM((2 * nh, nk, tn), F32),
                        pltpu.VMEM((2 * nh, PEER_TOPK, tn), F32), pltpu.VMEM((ncand_pad, LANES), F32)],
        compiler_params=_cparams("arbitrary"),
        name="peer_route",
    )(h, g.reshape(1, d), wq, sk)
    td = min(td, n)
    assert n % td == 0 and td % tw == 0
    nts = td // LANES
    return pl.pallas_call(
        functools.partial(_peer_dense_kernel, final_norm=final_norm),
        grid=(n // td, ne // ec),
        in_specs=[pl.BlockSpec((td, d), lambda i, c: (i, 0)),
                  pl.BlockSpec((td // tw, d, tw), lambda i, c: (i, 0, 0)),
                  pl.BlockSpec((ec, d), lambda i, c: (c, 0)),
                  pl.BlockSpec((ec // rb, d, rb), lambda i, c: (c, 0, 0)),
                  pl.BlockSpec((nh, nts, nk, LANES), lambda i, c: (0, i, 0, 0)),
                  pl.BlockSpec((nh, nts, nk, LANES), lambda i, c: (0, i, 0, 0)),
                  pl.BlockSpec((nh, nts, ic, LANES), lambda i, c: (0, i, c, 0)),
                  pl.BlockSpec((nh, nts, ic, LANES), lambda i, c: (0, i, c, 0)),
                  _full((1, d))],
        out_specs=pl.BlockSpec((td, d), lambda i, c: (i, 0)),
        out_shape=jax.ShapeDtypeStruct((n, d), F32),
        scratch_shapes=[pltpu.VMEM((rb, tw), F32)] * 2 + [pltpu.VMEM((rb, tw), BF16)] * 2
                       + [pltpu.VMEM((td // tw, d, tw), F32)] + [pltpu.VMEM((nh, nts, nk, LANES), BF16)] * 2,
        compiler_params=_cparams("arbitrary", "arbitrary"),
        name="peer_dense",
    )(h, xnt, u, vt, rk, eb, cnt, aw, gf.reshape(1, d))


def _shared_kv_kernel(h_ref, g_ref, wdkv_ref, gc_ref, wkr_ref, wkrs_ref, cos_ref, sin_ref,
                      c_ref, kr_ref, cb_ref, krb_ref):
    hn = _rms(h_ref[...], g_ref[...]).astype(BF16)
    c = _rms(jnp.dot(hn, wdkv_ref[...], preferred_element_type=F32), gc_ref[...])
    kr = (jnp.dot(hn, wkr_ref[...], preferred_element_type=F32) * cos_ref[...]
          + jnp.dot(hn, wkrs_ref[...], preferred_element_type=F32) * sin_ref[...])
    c_ref[...] = c
    kr_ref[...] = kr
    cb_ref[...] = c.astype(BF16)
    krb_ref[...] = kr.astype(BF16)


def _shared_kv(h, g, wdkv, gc, wkr, wkrs, cos, sin, *, tn):
    n, d = h.shape
    kvl, rr = wdkv.shape[1], wkr.shape[1]
    period = cos.shape[0] // tn
    tok = lambda i: (i, 0)
    return pl.pallas_call(
        _shared_kv_kernel,
        grid=(n // tn,),
        in_specs=[pl.BlockSpec((tn, d), tok), _full((1, d)), _full(wdkv.shape), _full((1, kvl)),
                  _full(wkr.shape), _full(wkrs.shape),
                  pl.BlockSpec((tn, rr), lambda i: (i % period, 0)),
                  pl.BlockSpec((tn, rr), lambda i: (i % period, 0))],
        out_specs=[pl.BlockSpec((tn, kvl), tok), pl.BlockSpec((tn, rr), tok),
                   pl.BlockSpec((tn, kvl), tok), pl.BlockSpec((tn, rr), tok)],
        out_shape=[jax.ShapeDtypeStruct((n, kvl), F32), jax.ShapeDtypeStruct((n, rr), F32),
                   jax.ShapeDtypeStruct((n, kvl), BF16), jax.ShapeDtypeStruct((n, rr), BF16)],
        compiler_params=_cparams("arbitrary"),
        name="shared_kv",
    )(h, g.reshape(1, d), wdkv, gc.reshape(1, kvl), wkr, wkrs, cos, sin)


def _mla_q_kernel(h_ref, g_ref, wdq_ref, gq_ref, wn_ref, wr_ref, wrs_ref, wuk_ref, cos_ref, sin_ref,
                  ql_ref, qr_ref, *, scale):
    xn = _rms(h_ref[...], g_ref[...]).astype(BF16)
    cq = _rms(jnp.dot(xn, wdq_ref[...], preferred_element_type=F32), gq_ref[...]).astype(BF16)
    qn = jnp.dot(cq, wn_ref[...], preferred_element_type=F32).astype(BF16)
    qr = (jnp.dot(cq, wr_ref[...], preferred_element_type=F32) * cos_ref[...]
          + jnp.dot(cq, wrs_ref[...], preferred_element_type=F32) * sin_ref[...])
    qr = (qr * scale).astype(BF16)
    nheads, _, rr = qr_ref.shape
    for h in range(nheads):
        qr_ref[h] = qr[:, h * rr:(h + 1) * rr]
    npair, pw, ow = wuk_ref.shape
    for p in range(npair):
        ql = jnp.dot(qn[:, p * pw:(p + 1) * pw], wuk_ref[p], preferred_element_type=F32)
        ql = (ql * scale).astype(BF16)
        ql_ref[2 * p] = ql[:, :ow // 2]
        ql_ref[2 * p + 1] = ql[:, ow // 2:]


def _mla_q(h, g, wdq, gq, wn, wr, wrs, wuk_bd, cos, sin, *, scale, tn):
    n, d = h.shape
    nheads = 2 * wuk_bd.shape[0]
    kvl = wuk_bd.shape[2] // 2
    qr_w = wr.shape[1]
    rr = qr_w // nheads
    period = cos.shape[0] // tn
    tok = lambda i: (i, 0)
    htok = lambda i: (0, i, 0)
    return pl.pallas_call(
        functools.partial(_mla_q_kernel, scale=scale),
        grid=(n // tn,),
        in_specs=[pl.BlockSpec((tn, d), tok), _full((1, d)), _full(wdq.shape), _full((1, wdq.shape[1])),
                  _full(wn.shape), _full(wr.shape), _full(wrs.shape), _full(wuk_bd.shape),
                  pl.BlockSpec((tn, qr_w), lambda i: (i % period, 0)),
                  pl.BlockSpec((tn, qr_w), lambda i: (i % period, 0))],
        out_specs=[pl.BlockSpec((nheads, tn, kvl), htok), pl.BlockSpec((nheads, tn, rr), htok)],
        out_shape=[jax.ShapeDtypeStruct((nheads, n, kvl), BF16), jax.ShapeDtypeStruct((nheads, n, rr), BF16)],
        compiler_params=_cparams("arbitrary"),
        name="mla_q",
    )(h, g.reshape(1, d), wdq, gq.reshape(1, -1), wn, wr, wrs, wuk_bd, cos, sin)


def _softmax_step(s, c, m_s, l_s, acc_s, rows=slice(None)):
    m_prev = m_s[rows]
    m_new = jnp.maximum(m_prev, jnp.max(s, axis=1, keepdims=True))
    alpha = jnp.exp(m_prev - m_new)
    p = jnp.exp(s - m_new)
    l_s[rows] = alpha * l_s[rows] + jnp.sum(p, axis=1, keepdims=True)
    acc_s[rows] = alpha * acc_s[rows] + jnp.dot(p.astype(BF16), c, preferred_element_type=F32)
    m_s[rows] = m_new


def _attn_prompt_kernel(ql_ref, qr_ref, c_ref, kr_ref, o_ref, m_s, l_s, acc_s, *, tq, tk, hc):
    qi, ki = pl.program_id(1), pl.program_id(2)
    nheads, _, kvl = ql_ref.shape
    rr = qr_ref.shape[2]
    q_lo, k_lo = qi * tq, ki * tk

    @pl.when(ki == 0)
    def _():
        m_s[...] = jnp.full(m_s.shape, NEG_INF, F32)
        l_s[...] = jnp.zeros(l_s.shape, F32)
        acc_s[...] = jnp.zeros(acc_s.shape, F32)

    def step(masked):
        c, kr = c_ref[0], kr_ref[0]
        rows = hc * tq
        for h0 in range(0, nheads, hc):
            ql = ql_ref[h0:h0 + hc].reshape(rows, kvl)
            qr = qr_ref[h0:h0 + hc].reshape(rows, rr)
            s = (lax.dot_general(ql, c, _NT, preferred_element_type=F32)
                 + lax.dot_general(qr, kr, _NT, preferred_element_type=F32))
            if masked:
                qpos = q_lo + (lax.broadcasted_iota(jnp.int32, (rows, tk), 0) & (tq - 1))
                kpos = k_lo + lax.broadcasted_iota(jnp.int32, (rows, tk), 1)
                s = jnp.where(kpos <= qpos, s, NEG_INF)
            _softmax_step(s, c, m_s, l_s, acc_s, slice(h0 * tq, (h0 + hc) * tq))

    @pl.when(k_lo + tk - 1 <= q_lo)
    def _():
        step(False)

    @pl.when((k_lo + tk - 1 > q_lo) & (k_lo <= q_lo + tq - 1))
    def _():
        step(True)

    @pl.when(ki == pl.num_programs(2) - 1)
    def _():
        o_ref[...] = (acc_s[...] / l_s[...]).astype(BF16).reshape(nheads, tq, kvl)


def _attn_prompt(ql, qr, cb, krb, *, tq=128, tk=512, hc=4):
    nheads, n, kvl = ql.shape
    rr = qr.shape[2]
    b, t, _ = cb.shape
    assert tq & (tq - 1) == 0 and t % tq == 0 and t % tk == 0 and nheads % hc == 0
    nq = t // tq
    rows = tq * nheads
    qmap = lambda i, q, k: (0, i * nq + q, 0)
    kmap = lambda i, q, k: (i, jnp.minimum(k, (q * tq + tq - 1) // tk), 0)
    return pl.pallas_call(
        functools.partial(_attn_prompt_kernel, tq=tq, tk=tk, hc=hc),
        grid=(b, nq, t // tk),
        in_specs=[pl.BlockSpec((nheads, tq, kvl), qmap), pl.BlockSpec((nheads, tq, rr), qmap),
                  pl.BlockSpec((1, tk, kvl), kmap), pl.BlockSpec((1, tk, rr), kmap)],
        out_specs=pl.BlockSpec((nheads, tq, kvl), qmap),
        out_shape=jax.ShapeDtypeStruct((nheads, n, kvl), BF16),
        scratch_shapes=[pltpu.VMEM((rows, 1), F32), pltpu.VMEM((rows, 1), F32), pltpu.VMEM((rows, kvl), F32)],
        compiler_params=_cparams("arbitrary", "arbitrary", "arbitrary"),
        name="attn_prompt",
    )(ql, qr, cb, krb)


def _attn_sample_kernel(pt_ref, ql_ref, qr_ref, cn_ref, krn_ref, *rest, npp, nheads, n_new):
    del pt_ref
    cpages, krpages = rest[:npp], rest[npp:2 * npp]
    o_ref, m_s, l_s, acc_s = rest[2 * npp:]
    g = pl.program_id(1)
    ql, qr = ql_ref[0], qr_ref[0]
    rows = ql.shape[0]

    @pl.when(g == 0)
    def _():
        m_s[...] = jnp.full(m_s.shape, NEG_INF, F32)
        l_s[...] = jnp.zeros(l_s.shape, F32)
        acc_s[...] = jnp.zeros(acc_s.shape, F32)
        c = cn_ref[0]
        s = (lax.dot_general(ql, c, _NT, preferred_element_type=F32)
             + lax.dot_general(qr, krn_ref[0], _NT, preferred_element_type=F32))
        qpos = lax.rem(lax.broadcasted_iota(jnp.int32, s.shape, 0), n_new)
        kpos = lax.broadcasted_iota(jnp.int32, s.shape, 1)
        s = jnp.where((kpos <= qpos) & (kpos < n_new), s, NEG_INF)
        _softmax_step(s, c, m_s, l_s, acc_s)

    c = jnp.concatenate([r[0].astype(BF16) for r in cpages], axis=0)
    krt = jnp.concatenate([r[0].astype(BF16) for r in krpages], axis=1)
    s = (lax.dot_general(ql, c, _NT, preferred_element_type=F32)
         + jnp.dot(qr, krt, preferred_element_type=F32))
    _softmax_step(s, c, m_s, l_s, acc_s)

    @pl.when(g == pl.num_programs(1) - 1)
    def _():
        o_ref[0] = (acc_s[...] / l_s[...]).astype(BF16)


def _attn_sample(ql, qr, cn, krn, cache_c, cache_krt, page_table, *, nheads, n_new, npp=32):
    b, rows, kvl = ql.shape
    rr = qr.shape[2]
    page = cache_c.shape[1]
    n_pages = page_table.shape[1]
    assert n_pages % npp == 0 and cn.shape[1] == page
    bmap = lambda i, g, pt: (i, 0, 0)
    pmap = lambda j: (lambda i, g, pt: (pt[i, g * npp + j], 0, 0))
    grid_spec = pltpu.PrefetchScalarGridSpec(
        num_scalar_prefetch=1,
        grid=(b, n_pages // npp),
        in_specs=[pl.BlockSpec((1, rows, kvl), bmap), pl.BlockSpec((1, rows, rr), bmap),
                  pl.BlockSpec((1, page, kvl), bmap), pl.BlockSpec((1, page, rr), bmap)]
                 + [pl.BlockSpec((1, page, kvl), pmap(j)) for j in range(npp)]
                 + [pl.BlockSpec((1, rr, page), pmap(j)) for j in range(npp)],
        out_specs=pl.BlockSpec((1, rows, kvl), bmap),
        scratch_shapes=[pltpu.VMEM((rows, 1), F32), pltpu.VMEM((rows, 1), F32), pltpu.VMEM((rows, kvl), F32)],
    )
    return pl.pallas_call(
        functools.partial(_attn_sample_kernel, npp=npp, nheads=nheads, n_new=n_new),
        grid_spec=grid_spec,
        out_shape=jax.ShapeDtypeStruct((b, rows, kvl), BF16),
        compiler_params=_cparams("arbitrary", "arbitrary"),
        name="attn_sample",
    )(page_table, ql, qr, cn, krn, *([cache_c] * npp), *([cache_krt] * npp))


def _mla_out_kernel(h_ref, o_ref, wuv_ref, wo_ref, y_ref):
    npair = wuv_ref.shape[0]
    parts = [jnp.dot(jnp.concatenate([o_ref[2 * p], o_ref[2 * p + 1]], axis=-1), wuv_ref[p],
                     preferred_element_type=F32).astype(BF16) for p in range(npair)]
    y_ref[...] = h_ref[...] + jnp.dot(jnp.concatenate(parts, axis=-1), wo_ref[...], preferred_element_type=F32)


def _mla_out(h, o, wuv_bd, wo, *, tn):
    n, d = h.shape
    nheads, _, kvl = o.shape
    tok = lambda i: (i, 0)
    return pl.pallas_call(
        _mla_out_kernel,
        grid=(n // tn,),
        in_specs=[pl.BlockSpec((tn, d), tok), pl.BlockSpec((nheads, tn, kvl), lambda i: (0, i, 0)),
                  _full(wuv_bd.shape), _full(wo.shape)],
        out_specs=pl.BlockSpec((tn, d), tok),
        out_shape=jax.ShapeDtypeStruct((n, d), F32),
        compiler_params=_cparams("arbitrary"),
        name="mla_out",
    )(h, o, wuv_bd, wo)


def _rope_tables(pos, half, reps):
    inv = ROPE_THETA ** (-jnp.arange(half, dtype=F32) / half)
    ang = pos.astype(F32)[:, None] * inv[None, :]
    cos, sin = jnp.cos(ang), jnp.sin(ang)
    return (jnp.tile(jnp.concatenate([cos, cos], axis=-1), (1, reps)),
            jnp.tile(jnp.concatenate([-sin, sin], axis=-1), (1, reps)))


def _swap_halves(w):
    half = w.shape[-1] // 2
    return jnp.concatenate([w[..., half:], w[..., :half]], axis=-1)


def _pair_block_diag(w):
    h, k, n = w.shape
    z = jnp.zeros((h // 2, k, n), w.dtype)
    top = jnp.concatenate([w[0::2], z], axis=2)
    bot = jnp.concatenate([z, w[1::2]], axis=2)
    return jnp.concatenate([top, bot], axis=1)


def kernel(x_prompt, x_sample, cache_ckv, cache_krope, page_table, state_h, state_conv, ln_mix, ln_ffn, ln_final, lru_w_in, lru_conv_w, lru_conv_b, lru_gate_a_w, lru_gate_a_b, lru_gate_x_w, lru_gate_x_b, lru_lambda, lru_w_out, kv_norm, w_dkv, ckv_norm, w_kr, w_uk, w_uv, mla_w_dq, mla_q_norm, mla_w_uq, mla_w_o, peer_w_q, peer_subkeys, peer_u, peer_v):
    bp, sp, d = x_prompt.shape
    bd, sd, _ = x_sample.shape
    depth = ln_mix.shape[0]
    assert depth == 2 and lru_w_in.shape[0] == 1 and mla_w_dq.shape[0] == 1
    kvl, nheads, qk_nope = w_uk.shape
    v_head = w_uv.shape[2]
    qk_rope = w_kr.shape[1]
    page = cache_ckv.shape[1]
    past_len = page_table.shape[1] * page
    scale = float(qk_nope + qk_rope) ** -0.5
    np_tok, nd_tok = bp * sp, bd * sd

    lw = (ln_mix[0], lru_w_in[0].astype(BF16), lru_conv_w[0], lru_conv_b[0],
          lru_gate_a_w[0].astype(BF16), lru_gate_a_b[0], lru_gate_x_w[0].astype(BF16), lru_gate_x_b[0],
          lru_lambda[0], lru_w_out[0].astype(BF16))
    hp, hl_p, cb_p = _lru_prompt(x_prompt, *lw)
    hd, hl_d, cb_d = _lru_sample(x_sample, state_conv[:, 0], state_h[:, 0], *lw)
    hp = hp.reshape(np_tok, d)
    hd = hd.reshape(nd_tok, d)

    nkeys = peer_subkeys.shape[3]
    rb = 4 * nkeys
    peer_tabs = [(peer_u[l].astype(BF16),
                  jnp.swapaxes(peer_v[l].astype(BF16).reshape(-1, rb, d), 1, 2)) for l in range(depth)]

    def peer(h, l, final_norm):
        return _peer(h, ln_ffn[l], peer_w_q[l].astype(BF16), peer_subkeys[l].astype(BF16),
                     *peer_tabs[l], ln_final, final_norm=final_norm)

    hp = peer(hp, 0, False)
    hd = peer(hd, 0, False)

    pos_p = jnp.arange(sp, dtype=jnp.int32)
    pos_d = past_len + jnp.arange(sd, dtype=jnp.int32)
    kvw = (kv_norm, w_dkv.astype(BF16), ckv_norm, w_kr.astype(BF16), _swap_halves(w_kr).astype(BF16))
    tn_p, tn_d = 512, nd_tok
    cos_p, sin_p = _rope_tables(pos_p, qk_rope // 2, 1)
    cos_d, sin_d = _rope_tables(jnp.tile(pos_d, bd), qk_rope // 2, 1)
    c_p, kr_p, cb16_p, krb16_p = _shared_kv(hp, *kvw, cos_p, sin_p, tn=tn_p)
    c_d, kr_d, cb16_d, krb16_d = _shared_kv(hd, *kvw, cos_d, sin_d, tn=tn_d)

    wuq = mla_w_uq[0].reshape(-1, nheads, qk_nope + qk_rope)
    w_nope = wuq[:, :, :qk_nope].reshape(-1, nheads * qk_nope).astype(BF16)
    w_rope = wuq[:, :, qk_nope:]
    w_rope_sw = _swap_halves(w_rope).reshape(-1, nheads * qk_rope).astype(BF16)
    w_rope = w_rope.reshape(-1, nheads * qk_rope).astype(BF16)
    wuk_bd = _pair_block_diag(jnp.transpose(w_uk, (1, 2, 0))).astype(BF16)
    wuv_bd = _pair_block_diag(jnp.transpose(w_uv, (1, 0, 2))).astype(BF16)
    qw = (ln_mix[1], mla_w_dq[0].astype(BF16), mla_q_norm[0], w_nope, w_rope, w_rope_sw, wuk_bd)
    ql_p, qr_p = _mla_q(hp, *qw, jnp.tile(cos_p, (1, nheads)), jnp.tile(sin_p, (1, nheads)), scale=scale, tn=tn_p)
    ql_d, qr_d = _mla_q(hd, *qw, jnp.tile(cos_d, (1, nheads)), jnp.tile(sin_d, (1, nheads)), scale=scale, tn=tn_d)

    o_p = _attn_prompt(ql_p, qr_p, cb16_p.reshape(bp, sp, kvl), krb16_p.reshape(bp, sp, qk_rope))

    def per_batch(a):
        return jnp.transpose(a.reshape(nheads, bd, sd, -1), (1, 0, 2, 3)).reshape(bd, nheads * sd, -1)

    pad = ((0, 0), (0, page - sd), (0, 0))
    o_d = _attn_sample(per_batch(ql_d), per_batch(qr_d),
                       jnp.pad(cb16_d.reshape(bd, sd, kvl), pad), jnp.pad(krb16_d.reshape(bd, sd, qk_rope), pad),
                       cache_ckv, jnp.swapaxes(cache_krope, 1, 2), page_table, nheads=nheads, n_new=sd)
    o_d = jnp.transpose(o_d.reshape(bd, nheads, sd, kvl), (1, 0, 2, 3)).reshape(nheads, nd_tok, kvl)
    wo = mla_w_o[0].astype(BF16)
    hp = _mla_out(hp, o_p, wuv_bd, wo, tn=tn_p)
    hd = _mla_out(hd, o_d, wuv_bd, wo, tn=tn_d)

    y_p = peer(hp, 1, True)
    y_d = peer(hd, 1, True)

    return (y_p.reshape(bp, sp, d), y_d.reshape(bd, sd, d),
            hl_p, cb_p[:, None], c_p.reshape(bp, sp, kvl), kr_p.reshape(bp, sp, qk_rope),
            hl_d[:, None], cb_d[:, None], c_d.reshape(bd, sd, kvl), kr_d.reshape(bd, sd, qk_rope))
```

```python
import functools

import jax
import jax.numpy as jnp
from jax import lax
from jax.experimental import pallas as pl
from jax.experimental.pallas import tpu as pltpu

F32 = jnp.float32
BF16 = jnp.bfloat16

RMS_EPS = 1e-6
NEG_INF = -1e30
LRU_C = 8.0
CONV_W = 4
ROPE_THETA = 10000.0
PEER_TOPK = 16

LANES = 128
SUBLANES = 8
VMEM_LIMIT = 56 * 1024 * 1024

_NT = (((1,), (1,)), ((), ()))


def _cparams(*sem):
    return pltpu.CompilerParams(dimension_semantics=sem, vmem_limit_bytes=VMEM_LIMIT)


def _rms(x, g):
    return x * lax.rsqrt(jnp.mean(x * x, axis=-1, keepdims=True) + RMS_EPS) * g


def _gelu(x):
    inner = x * (0.7978845608028654 + (0.7978845608028654 * 0.044715) * (x * x))
    return x * (0.5 + 0.5 * jnp.tanh(inner))


def _sigmoid(x):
    return 1.0 / (1.0 + jnp.exp(-x))


def _softplus(x):
    return jnp.maximum(x, 0.0) + jnp.log(1.0 + jnp.exp(-jnp.abs(x)))


def _full(shape):
    n = len(shape)
    return pl.BlockSpec(shape, lambda *_: (0,) * n)


def _lru_gates(xc, gaw_ref, gab_ref, gxw_ref, gxb_ref, lam_ref):
    nh, bw, _ = gaw_ref.shape
    xcb = xc.astype(BF16)
    r_parts, i_parts = [], []
    for h in range(nh):
        blk = xcb[:, h * bw:(h + 1) * bw]
        r_parts.append(jnp.dot(blk, gaw_ref[h], preferred_element_type=F32))
        i_parts.append(jnp.dot(blk, gxw_ref[h], preferred_element_type=F32))
    r = _sigmoid(jnp.concatenate(r_parts, axis=-1) + gab_ref[...])
    i = _sigmoid(jnp.concatenate(i_parts, axis=-1) + gxb_ref[...])
    log_a = (-LRU_C) * r * _softplus(-lam_ref[...])
    a = jnp.exp(log_a)
    bx = jnp.sqrt(1.0 - jnp.exp(2.0 * log_a)) * i * xc
    return a, bx


def _lru_prompt_kernel(x_ref, g_ref, win_ref, cw_ref, cb_ref, gaw_ref, gab_ref, gxw_ref, gxb_ref,
                       lam_ref, wout_ref, y_ref, hl_ref, cbuf_ref, xcat, a_s, b_s, h_s, hcar):
    t = pl.program_id(1)
    tt, d = x_ref.shape[1], x_ref.shape[2]

    @pl.when(t == 0)
    def _():
        xcat[0:SUBLANES, :] = jnp.zeros((SUBLANES, d), F32)
        hcar[...] = jnp.zeros((1, d), F32)

    x = x_ref[0]
    xn = _rms(x, g_ref[...])
    proj = jnp.dot(xn.astype(BF16), win_ref[...], preferred_element_type=F32)
    gate = _gelu(proj[:, :d])
    xb = proj[:, d:]
    xcat[SUBLANES:SUBLANES + tt, :] = xb
    xc = (cb_ref[...]
          + cw_ref[0:1, :] * xcat[SUBLANES - 3:SUBLANES - 3 + tt, :]
          + cw_ref[1:2, :] * xcat[SUBLANES - 2:SUBLANES - 2 + tt, :]
          + cw_ref[2:3, :] * xcat[SUBLANES - 1:SUBLANES - 1 + tt, :]
          + cw_ref[3:4, :] * xb)
    tail = xcat[tt:tt + SUBLANES, :]
    xcat[0:SUBLANES, :] = tail
    cbuf_ref[0] = tail

    a, bx = _lru_gates(xc, gaw_ref, gab_ref, gxw_ref, gxb_ref, lam_ref)
    a_s[...] = a
    b_s[...] = bx

    def body(i, h):
        h = a_s[pl.ds(i, 1), :] * h + b_s[pl.ds(i, 1), :]
        h_s[pl.ds(i, 1), :] = h
        return h

    h = lax.fori_loop(0, tt, body, hcar[...], unroll=8)
    hcar[...] = h
    hl_ref[0] = h
    y = jnp.dot((h_s[...] * gate).astype(BF16), wout_ref[...], preferred_element_type=F32)
    y_ref[0] = x + y


def _lru_prompt(x, g, win, cw, cb, gaw, gab, gxw, gxb, lam, wout, *, tt=256):
    b, t, d = x.shape
    nh, bw, _ = gaw.shape
    row = lambda v: v.reshape(1, -1)
    wspecs = [_full((1, d)), _full((d, 2 * d)), _full((CONV_W, d)), _full((1, d)),
              _full((nh, bw, bw)), _full((1, d)), _full((nh, bw, bw)), _full((1, d)),
              _full((1, d)), _full((d, d))]
    y, hl, cbuf = pl.pallas_call(
        _lru_prompt_kernel,
        grid=(b, t // tt),
        in_specs=[pl.BlockSpec((1, tt, d), lambda i, j: (i, j, 0))] + wspecs,
        out_specs=[pl.BlockSpec((1, tt, d), lambda i, j: (i, j, 0)),
                   pl.BlockSpec((1, 1, d), lambda i, j: (i, 0, 0)),
                   pl.BlockSpec((1, SUBLANES, d), lambda i, j: (i, 0, 0))],
        out_shape=[jax.ShapeDtypeStruct((b, t, d), F32),
                   jax.ShapeDtypeStruct((b, 1, d), F32),
                   jax.ShapeDtypeStruct((b, SUBLANES, d), F32)],
        scratch_shapes=[pltpu.VMEM((tt + SUBLANES, d), F32), pltpu.VMEM((tt, d), F32),
                        pltpu.VMEM((tt, d), F32), pltpu.VMEM((tt, d), F32), pltpu.VMEM((1, d), F32)],
        compiler_params=_cparams("arbitrary", "arbitrary"),
        name="lru_prompt",
    )(x, row(g), win, cw, row(cb), gaw, row(gab), gxw, row(gxb), row(lam), wout)
    return y, hl, cbuf[:, SUBLANES - (CONV_W - 1):, :]


def _lru_sample_kernel(x_ref, conv_ref, h0_ref, g_ref, win_ref, cw_ref, cb_ref, gaw_ref, gab_ref,
                       gxw_ref, gxb_ref, lam_ref, wout_ref, y_ref, hl_ref, cbuf_ref):
    t, b, d = x_ref.shape
    x = x_ref[...].reshape(t * b, d)
    xn = _rms(x, g_ref[...])
    proj = jnp.dot(xn.astype(BF16), win_ref[...], preferred_element_type=F32)
    gate = _gelu(proj[:, :d])
    xb = proj[:, d:]
    seq = [conv_ref[k] for k in range(CONV_W - 1)] + [xb[k * b:(k + 1) * b, :] for k in range(t)]
    xc_parts = []
    for k in range(t):
        acc = cb_ref[...] + cw_ref[0:1, :] * seq[k]
        for j in range(1, CONV_W):
            acc = acc + cw_ref[j:j + 1, :] * seq[k + j]
        xc_parts.append(acc)
    for k in range(CONV_W - 1):
        cbuf_ref[k] = seq[t + k]
    xc = jnp.concatenate(xc_parts, axis=0)
    a, bx = _lru_gates(xc, gaw_ref, gab_ref, gxw_ref, gxb_ref, lam_ref)
    h = h0_ref[...]
    hs = []
    for k in range(t):
        h = a[k * b:(k + 1) * b, :] * h + bx[k * b:(k + 1) * b, :]
        hs.append(h)
    hl_ref[...] = h
    hall = jnp.concatenate(hs, axis=0)
    y = jnp.dot((hall * gate).astype(BF16), wout_ref[...], preferred_element_type=F32)
    y_ref[...] = (x + y).reshape(t, b, d)


def _lru_sample(x, conv, h0, g, win, cw, cb, gaw, gab, gxw, gxb, lam, wout):
    b, t, d = x.shape
    assert t >= CONV_W - 1
    row = lambda v: v.reshape(1, -1)
    y, hl, cbuf = pl.pallas_call(
        _lru_sample_kernel,
        out_shape=[jax.ShapeDtypeStruct((t, b, d), F32),
                   jax.ShapeDtypeStruct((b, d), F32),
                   jax.ShapeDtypeStruct((CONV_W - 1, b, d), F32)],
        compiler_params=pltpu.CompilerParams(vmem_limit_bytes=VMEM_LIMIT),
        name="lru_sample",
    )(jnp.swapaxes(x, 0, 1), jnp.swapaxes(conv, 0, 1), h0, row(g), win, cw, row(cb),
      gaw, row(gab), gxw, row(gxb), row(lam), wout)
    return jnp.swapaxes(y, 0, 1), hl, jnp.swapaxes(cbuf, 0, 1)


def _cand_pairs(k):
    return [(a, min(k, (k + 1) // (a + 1))) for a in range(k) if (k + 1) // (a + 1) >= 1]


def _peer_route_kernel(h_ref, g_ref, wq_ref, sk_ref, xnt_ref, rk_ref, eb_ref, cnt_ref, aw_ref,
                       q_s, st_s, sv_s, cand_s):
    tn, d = h_ref.shape
    nh, _, nk, dk2 = sk_ref.shape
    xn = _rms(h_ref[...], g_ref[...])
    xnt = xn.T.astype(BF16)
    tw = xnt_ref.shape[2]
    for k in range(xnt_ref.shape[0]):
        xnt_ref[k] = xnt[:, k * tw:(k + 1) * tw]
    q = jnp.dot(xn.astype(BF16), wq_ref[...], preferred_element_type=F32).astype(BF16)
    for hp in range(2 * nh):
        q_s[hp] = q[:, hp * dk2:(hp + 1) * dk2]

    def scores(hp, carry):
        s = lax.dot_general(sk_ref[hp // 2, hp % 2], q_s[hp], _NT, preferred_element_type=F32)
        st_s[hp] = s
        for ts in range(tn // LANES):
            cur = s[:, ts * LANES:(ts + 1) * LANES]
            for r in range(PEER_TOPK):
                m = jnp.max(cur, axis=0, keepdims=True)
                sv_s[hp, r:r + 1, ts * LANES:(ts + 1) * LANES] = m
                cur = jnp.where(cur == m, -jnp.inf, cur)
        return carry

    lax.fori_loop(0, 2 * nh, scores, 0)

    pairs = _cand_pairs(PEER_TOPK)
    ncand = sum(nb for _, nb in pairs)
    cand_s[...] = jnp.full(cand_s.shape, -jnp.inf, F32)

    def thresholds(h, carry):
        for ts in range(tn // LANES):
            sl = slice(ts * LANES, (ts + 1) * LANES)
            sv0 = sv_s[2 * h, :, sl]
            sv1 = sv_s[2 * h + 1, :, sl]
            off = 0
            for a, nb in pairs:
                cand_s[off:off + nb, :] = sv0[a:a + 1, :] + sv1[0:nb, :]
                off += nb
            cand0 = cand_s[...]
            cur = cand0
            t_hi = None
            for r in range(PEER_TOPK + 1):
                m = jnp.max(cur, axis=0, keepdims=True)
                if r == PEER_TOPK - 1:
                    t_hi = m
                cur = jnp.where(cur == m, -jnp.inf, cur)
            t_mid = 0.5 * (t_hi + m)
            m0 = sv0[0:1, :]
            m1 = sv1[0:1, :]
            z = jnp.sum(jnp.where(cand0 >= t_mid, jnp.exp(cand0 - (m0 + m1)), 0.0), axis=0, keepdims=True)
            s0 = st_s[2 * h, :, sl]
            s1 = st_s[2 * h + 1, :, sl]
            t0 = t_mid - s0
            rank = jnp.zeros(s1.shape, F32)
            count = jnp.zeros(s0.shape, F32)
            for r in range(PEER_TOPK):
                v = sv1[r:r + 1, :]
                rank = jnp.where(s1 < v, r + 1.0, rank)
                count = jnp.where(v >= t0, r + 1.0, count)
            rk_ref[h, ts] = rank.astype(BF16)
            eb_ref[h, ts] = jnp.exp(s1 - m1).astype(BF16)
            cnt_ref[h, ts] = _dup_bf16(count)
            aw_ref[h, ts] = _dup_bf16(jnp.exp(s0 - m0) * (1.0 / z))
        return carry

    assert ncand <= cand_s.shape[0]
    lax.fori_loop(0, nh, thresholds, 0)


def _row_bcast(ref, h, ts, i):
    return ref[h, ts, pl.ds(i, SUBLANES, stride=0), :]


def _dup_bf16(x):
    u = lax.bitcast_convert_type(x.astype(BF16).astype(F32), jnp.uint32)
    return u | (u >> 16)


def _row_bcast_bf16(ref, h, ts, i):
    return pltpu.bitcast(_row_bcast(ref, h, ts, i), BF16)


def _peer_dense_kernel(h_ref, xnt_ref, u_ref, vt_ref, rk_ref, eb_ref, cnt_ref, aw_ref, gf_ref, o_ref,
                       act_a, act_b, p_a, p_b, acc_s, rk_s, eb_s, *, final_norm):
    c = pl.program_id(1)
    nh, _, ic, _ = cnt_ref.shape
    nk = rk_ref.shape[2]
    ntw, _, tw = xnt_ref.shape
    nrg, _, rb = vt_ref.shape
    rg = rb // nk
    jb = min(64, nk)
    n_it = ntw * nrg

    @pl.when(c == 0)
    def _():
        acc_s[...] = jnp.zeros(acc_s.shape, F32)
        rk_s[...] = rk_ref[...]
        eb_s[...] = eb_ref[...]

    def act_dot(it):
        rows = pl.ds((it % nrg) * rb, rb)
        return jnp.dot(u_ref[rows, :], xnt_ref[it // nrg], preferred_element_type=F32)

    def out_dot(it, p_ref):
        acc_s[it // nrg] += jnp.dot(vt_ref[it % nrg], p_ref[...], preferred_element_type=F32)

    def gates(it, act_ref, p_ref, half):
        i8 = pl.ds(((it % nrg) // 2) * SUBLANES, SUBLANES)
        reps = jb // (2 * SUBLANES)

        def row_tile(tile8, row):
            x = jnp.broadcast_to(tile8[row:row + 1, :], (SUBLANES, LANES))
            return jnp.concatenate([pltpu.bitcast(x, BF16)] * reps, axis=0)

        for tsub in range(tw // LANES):
            ts = (it // nrg) * (tw // LANES) + tsub
            sl = slice(tsub * LANES, (tsub + 1) * LANES)
            cnt8 = [cnt_ref[h, ts, i8, :] for h in range(nh)]
            aw8 = [aw_ref[h, ts, i8, :] for h in range(nh)]
            for j0 in range(0, nk, jb):
                for r in range(rg):
                    g = jnp.zeros((jb, LANES), BF16)
                    for h in range(nh):
                        rk = rk_s[h, ts, j0:j0 + jb, :]
                        eb = eb_s[h, ts, j0:j0 + jb, :]
                        cnt = row_tile(cnt8[h], half * rg + r)
                        aw = row_tile(aw8[h], half * rg + r)
                        g = g + jnp.where(rk < cnt, eb, jnp.zeros_like(eb)) * aw
                    es = slice(r * nk + j0, r * nk + j0 + jb)
                    p_ref[es, sl] = _gelu(act_ref[es, sl]).astype(BF16) * g

    act_a[...] = act_dot(0)

    assert nrg % 2 == 0 and 2 * rg == SUBLANES
    for it in range(0, n_it, 2):
        act_b[...] = act_dot(it + 1)
        gates(it, act_a, p_a, 0)
        if it > 0:
            out_dot(it - 1, p_b)
        if it + 2 < n_it:
            act_a[...] = act_dot(it + 2)
        gates(it + 1, act_b, p_b, 1)
        out_dot(it, p_a)
    out_dot(n_it - 1, p_b)

    @pl.when(c == pl.num_programs(1) - 1)
    def _():
        out = h_ref[...] + jnp.concatenate([acc_s[k].T for k in range(ntw)], axis=0)
        if final_norm:
            out = _rms(out, gf_ref[...])
        o_ref[...] = out


def _peer(h, g, wq, sk, u, vt, gf, *, final_norm, tn=512, td=1024, tw=256, ec=1024):
    n, d = h.shape
    nh, _, nk, dk2 = sk.shape
    ne = u.shape[0]
    rb = vt.shape[2]
    ec = min(ec, ne)
    assert ne == nk * nk and n % tn == 0 and ne % ec == 0 and ec % (nk * SUBLANES) == 0
    assert tn % tw == 0 and tw % LANES == 0 and rb % nk == 0 and ec % rb == 0
    ic = ec // nk
    ncand_pad = 56
    nts = tn // LANES
    rt_spec = pl.BlockSpec((nh, nts, nk, LANES), lambda i: (0, i, 0, 0))
    rt_shape = lambda dt: jax.ShapeDtypeStruct((nh, n // LANES, nk, LANES), dt)
    xnt, rk, eb, cnt, aw = pl.pallas_call(
        _peer_route_kernel,
        grid=(n // tn,),
        in_specs=[pl.BlockSpec((tn, d), lambda i: (i, 0)), _full((1, d)), _full(wq.shape), _full(sk.shape)],
        out_specs=[pl.BlockSpec((tn // tw, d, tw), lambda i: (i, 0, 0))] + [rt_spec] * 4,
        out_shape=[jax.ShapeDtypeStruct((n // tw, d, tw), BF16),
                   rt_shape(BF16), rt_shape(BF16), rt_shape(jnp.uint32), rt_shape(jnp.uint32)],
        scratch_shapes=[pltpu.VMEM((2 * nh, tn, dk2), BF16), pltpu.VMEM((2 * nh, nk, tn), F32),
                        pltpu.VMEM((2 * nh, PEER_TOPK, tn), F32), pltpu.VMEM((ncand_pad, LANES), F32)],
        compiler_params=_cparams("arbitrary"),
        name="peer_route",
    )(h, g.reshape(1, d), wq, sk)
    td = min(td, n)
    assert n % td == 0 and td % tw == 0
    nts = td // LANES
    return pl.pallas_call(
        functools.partial(_peer_dense_kernel, final_norm=final_norm),
        grid=(n // td, ne // ec),
        in_specs=[pl.BlockSpec((td, d), lambda i, c: (i, 0)),
                  pl.BlockSpec((td // tw, d, tw), lambda i, c: (i, 0, 0)),
                  pl.BlockSpec((ec, d), lambda i, c: (c, 0)),
                  pl.BlockSpec((ec // rb, d, rb), lambda i, c: (c, 0, 0)),
                  pl.BlockSpec((nh, nts, nk, LANES), lambda i, c: (0, i, 0, 0)),
                  pl.BlockSpec((nh, nts, nk, LANES), lambda i, c: (0, i, 0, 0)),
                  pl.BlockSpec((nh, nts, ic, LANES), lambda i, c: (0, i, c, 0)),
                  pl.BlockSpec((nh, nts, ic, LANES), lambda i, c: (0, i, c, 0)),
                  _full((1, d))],
        out_specs=pl.BlockSpec((td, d), lambda i, c: (i, 0)),
        out_shape=jax.ShapeDtypeStruct((n, d), F32),
        scratch_shapes=[pltpu.VMEM((rb, tw), F32)] * 2 + [pltpu.VMEM((rb, tw), BF16)] * 2
                       + [pltpu.VMEM((td // tw, d, tw), F32)] + [pltpu.VMEM((nh, nts, nk, LANES), BF16)] * 2,
        compiler_params=_cparams("arbitrary", "arbitrary"),
        name="peer_dense",
    )(h, xnt, u, vt, rk, eb, cnt, aw, gf.reshape(1, d))


def _shared_kv_kernel(h_ref, g_ref, wdkv_ref, gc_ref, wkr_ref, wkrs_ref, cos_ref, sin_ref,
                      c_ref, kr_ref, cb_ref, krb_ref):
    hn = _rms(h_ref[...], g_ref[...]).astype(BF16)
    c = _rms(jnp.dot(hn, wdkv_ref[...], preferred_element_type=F32), gc_ref[...])
    kr = (jnp.dot(hn, wkr_ref[...], preferred_element_type=F32) * cos_ref[...]
          + jnp.dot(hn, wkrs_ref[...], preferred_element_type=F32) * sin_ref[...])
    c_ref[...] = c
    kr_ref[...] = kr
    cb_ref[...] = c.astype(BF16)
    krb_ref[...] = kr.astype(BF16)


def _shared_kv(h, g, wdkv, gc, wkr, wkrs, cos, sin, *, tn):
    n, d = h.shape
    kvl, rr = wdkv.shape[1], wkr.shape[1]
    period = cos.shape[0] // tn
    tok = lambda i: (i, 0)
    return pl.pallas_call(
        _shared_kv_kernel,
        grid=(n // tn,),
        in_specs=[pl.BlockSpec((tn, d), tok), _full((1, d)), _full(wdkv.shape), _full((1, kvl)),
                  _full(wkr.shape), _full(wkrs.shape),
                  pl.BlockSpec((tn, rr), lambda i: (i % period, 0)),
                  pl.BlockSpec((tn, rr), lambda i: (i % period, 0))],
        out_specs=[pl.BlockSpec((tn, kvl), tok), pl.BlockSpec((tn, rr), tok),
                   pl.BlockSpec((tn, kvl), tok), pl.BlockSpec((tn, rr), tok)],
        out_shape=[jax.ShapeDtypeStruct((n, kvl), F32), jax.ShapeDtypeStruct((n, rr), F32),
                   jax.ShapeDtypeStruct((n, kvl), BF16), jax.ShapeDtypeStruct((n, rr), BF16)],
        compiler_params=_cparams("arbitrary"),
        name="shared_kv",
    )(h, g.reshape(1, d), wdkv, gc.reshape(1, kvl), wkr, wkrs, cos, sin)


def _mla_q_kernel(h_ref, g_ref, wdq_ref, gq_ref, wn_ref, wr_ref, wrs_ref, wuk_ref, cos_ref, sin_ref,
                  ql_ref, qr_ref, *, scale):
    xn = _rms(h_ref[...], g_ref[...]).astype(BF16)
    cq = _rms(jnp.dot(xn, wdq_ref[...], preferred_element_type=F32), gq_ref[...]).astype(BF16)
    qn = jnp.dot(cq, wn_ref[...], preferred_element_type=F32).astype(BF16)
    qr = (jnp.dot(cq, wr_ref[...], preferred_element_type=F32) * cos_ref[...]
          + jnp.dot(cq, wrs_ref[...], preferred_element_type=F32) * sin_ref[...])
    qr = (qr * scale).astype(BF16)
    nheads, _, rr = qr_ref.shape
    for h in range(nheads):
        qr_ref[h] = qr[:, h * rr:(h + 1) * rr]
    npair, pw, ow = wuk_ref.shape
    for p in range(npair):
        ql = jnp.dot(qn[:, p * pw:(p + 1) * pw], wuk_ref[p], preferred_element_type=F32)
        ql = (ql * scale).astype(BF16)
        ql_ref[2 * p] = ql[:, :ow // 2]
        ql_ref[2 * p + 1] = ql[:, ow // 2:]


def _mla_q(h, g, wdq, gq, wn, wr, wrs, wuk_bd, cos, sin, *, scale, tn):
    n, d = h.shape
    nheads = 2 * wuk_bd.shape[0]
    kvl = wuk_bd.shape[2] // 2
    qr_w = wr.shape[1]
    rr = qr_w // nheads
    period = cos.shape[0] // tn
    tok = lambda i: (i, 0)
    htok = lambda i: (0, i, 0)
    return pl.pallas_call(
        functools.partial(_mla_q_kernel, scale=scale),
        grid=(n // tn,),
        in_specs=[pl.BlockSpec((tn, d), tok), _full((1, d)), _full(wdq.shape), _full((1, wdq.shape[1])),
                  _full(wn.shape), _full(wr.shape), _full(wrs.shape), _full(wuk_bd.shape),
                  pl.BlockSpec((tn, qr_w), lambda i: (i % period, 0)),
                  pl.BlockSpec((tn, qr_w), lambda i: (i % period, 0))],
        out_specs=[pl.BlockSpec((nheads, tn, kvl), htok), pl.BlockSpec((nheads, tn, rr), htok)],
        out_shape=[jax.ShapeDtypeStruct((nheads, n, kvl), BF16), jax.ShapeDtypeStruct((nheads, n, rr), BF16)],
        compiler_params=_cparams("arbitrary"),
        name="mla_q",
    )(h, g.reshape(1, d), wdq, gq.reshape(1, -1), wn, wr, wrs, wuk_bd, cos, sin)


def _softmax_step(s, c, m_s, l_s, acc_s, rows=slice(None)):
    m_prev = m_s[rows]
    m_new = jnp.maximum(m_prev, jnp.max(s, axis=1, keepdims=True))
    alpha = jnp.exp(m_prev - m_new)
    p = jnp.exp(s - m_new)
    l_s[rows] = alpha * l_s[rows] + jnp.sum(p, axis=1, keepdims=True)
    acc_s[rows] = alpha * acc_s[rows] + jnp.dot(p.astype(BF16), c, preferred_element_type=F32)
    m_s[rows] = m_new


def _attn_prompt_kernel(ql_ref, qr_ref, c_ref, kr_ref, o_ref, m_s, l_s, acc_s, *, tq, tk, hc):
    qi, ki = pl.program_id(1), pl.program_id(2)
    nheads, _, kvl = ql_ref.shape
    rr = qr_ref.shape[2]
    q_lo, k_lo = qi * tq, ki * tk

    @pl.when(ki == 0)
    def _():
        m_s[...] = jnp.full(m_s.shape, NEG_INF, F32)
        l_s[...] = jnp.zeros(l_s.shape, F32)
        acc_s[...] = jnp.zeros(acc_s.shape, F32)

    def step(masked):
        c, kr = c_ref[0], kr_ref[0]
        rows = hc * tq
        for h0 in range(0, nheads, hc):
            ql = ql_ref[h0:h0 + hc].reshape(rows, kvl)
            qr = qr_ref[h0:h0 + hc].reshape(rows, rr)
            s = (lax.dot_general(ql, c, _NT, preferred_element_type=F32)
                 + lax.dot_general(qr, kr, _NT, preferred_element_type=F32))
            if masked:
                qpos = q_lo + (lax.broadcasted_iota(jnp.int32, (rows, tk), 0) & (tq - 1))
                kpos = k_lo + lax.broadcasted_iota(jnp.int32, (rows, tk), 1)
                s = jnp.where(kpos <= qpos, s, NEG_INF)
            _softmax_step(s, c, m_s, l_s, acc_s, slice(h0 * tq, (h0 + hc) * tq))

    @pl.when(k_lo + tk - 1 <= q_lo)
    def _():
        step(False)

    @pl.when((k_lo + tk - 1 > q_lo) & (k_lo <= q_lo + tq - 1))
    def _():
        step(True)

    @pl.when(ki == pl.num_programs(2) - 1)
    def _():
        o_ref[...] = (acc_s[...] / l_s[...]).astype(BF16).reshape(nheads, tq, kvl)


def _attn_prompt(ql, qr, cb, krb, *, tq=128, tk=512, hc=4):
    nheads, n, kvl = ql.shape
    rr = qr.shape[2]
    b, t, _ = cb.shape
    assert tq & (tq - 1) == 0 and t % tq == 0 and t % tk == 0 and nheads % hc == 0
    nq = t // tq
    rows = tq * nheads
    qmap = lambda i, q, k: (0, i * nq + q, 0)
    kmap = lambda i, q, k: (i, jnp.minimum(k, (q * tq + tq - 1) // tk), 0)
    return pl.pallas_call(
        functools.partial(_attn_prompt_kernel, tq=tq, tk=tk, hc=hc),
        grid=(b, nq, t // tk),
        in_specs=[pl.BlockSpec((nheads, tq, kvl), qmap), pl.BlockSpec((nheads, tq, rr), qmap),
                  pl.BlockSpec((1, tk, kvl), kmap), pl.BlockSpec((1, tk, rr), kmap)],
        out_specs=pl.BlockSpec((nheads, tq, kvl), qmap),
        out_shape=jax.ShapeDtypeStruct((nheads, n, kvl), BF16),
        scratch_shapes=[pltpu.VMEM((rows, 1), F32), pltpu.VMEM((rows, 1), F32), pltpu.VMEM((rows, kvl), F32)],
        compiler_params=_cparams("arbitrary", "arbitrary", "arbitrary"),
        name="attn_prompt",
    )(ql, qr, cb, krb)


def _attn_sample_kernel(pt_ref, ql_ref, qr_ref, cn_ref, krn_ref, *rest, npp, nheads, n_new):
    del pt_ref
    cpages, krpages = rest[:npp], rest[npp:2 * npp]
    o_ref, m_s, l_s, acc_s = rest[2 * npp:]
    g = pl.program_id(1)
    ql, qr = ql_ref[0], qr_ref[0]
    rows = ql.shape[0]

    @pl.when(g == 0)
    def _():
        m_s[...] = jnp.full(m_s.shape, NEG_INF, F32)
        l_s[...] = jnp.zeros(l_s.shape, F32)
        acc_s[...] = jnp.zeros(acc_s.shape, F32)
        c = cn_ref[0]
        s = (lax.dot_general(ql, c, _NT, preferred_element_type=F32)
             + lax.dot_general(qr, krn_ref[0], _NT, preferred_element_type=F32))
        qpos = lax.rem(lax.broadcasted_iota(jnp.int32, s.shape, 0), n_new)
        kpos = lax.broadcasted_iota(jnp.int32, s.shape, 1)
        s = jnp.where((kpos <= qpos) & (kpos < n_new), s, NEG_INF)
        _softmax_step(s, c, m_s, l_s, acc_s)

    c = jnp.concatenate([r[0].astype(BF16) for r in cpages], axis=0)
    krt = jnp.concatenate([r[0].astype(BF16) for r in krpages], axis=1)
    s = (lax.dot_general(ql, c, _NT, preferred_element_type=F32)
         + jnp.dot(qr, krt, preferred_element_type=F32))
    _softmax_step(s, c, m_s, l_s, acc_s)

    @pl.when(g == pl.num_programs(1) - 1)
    def _():
        o_ref[0] = (acc_s[...] / l_s[...]).astype(BF16)


def _attn_sample(ql, qr, cn, krn, cache_c, cache_krt, page_table, *, nheads, n_new, npp=32):
    b, rows, kvl = ql.shape
    rr = qr.shape[2]
    page = cache_c.shape[1]
    n_pages = page_table.shape[1]
    assert n_pages % npp == 0 and cn.shape[1] == page
    bmap = lambda i, g, pt: (i, 0, 0)
    pmap = lambda j: (lambda i, g, pt: (pt[i, g * npp + j], 0, 0))
    grid_spec = pltpu.PrefetchScalarGridSpec(
        num_scalar_prefetch=1,
        grid=(b, n_pages // npp),
        in_specs=[pl.BlockSpec((1, rows, kvl), bmap), pl.BlockSpec((1, rows, rr), bmap),
                  pl.BlockSpec((1, page, kvl), bmap), pl.BlockSpec((1, page, rr), bmap)]
                 + [pl.BlockSpec((1, page, kvl), pmap(j)) for j in range(npp)]
                 + [pl.BlockSpec((1, rr, page), pmap(j)) for j in range(npp)],
        out_specs=pl.BlockSpec((1, rows, kvl), bmap),
        scratch_shapes=[pltpu.VMEM((rows, 1), F32), pltpu.VMEM((rows, 1), F32), pltpu.VMEM((rows, kvl), F32)],
    )
    return pl.pallas_call(
        functools.partial(_attn_sample_kernel, npp=npp, nheads=nheads, n_new=n_new),
        grid_spec=grid_spec,
        out_shape=jax.ShapeDtypeStruct((b, rows, kvl), BF16),
        compiler_params=_cparams("arbitrary", "arbitrary"),
        name="attn_sample",
    )(page_table, ql, qr, cn, krn, *([cache_c] * npp), *([cache_krt] * npp))


def _mla_out_kernel(h_ref, o_ref, wuv_ref, wo_ref, y_ref):
    npair = wuv_ref.shape[0]
    parts = [jnp.dot(jnp.concatenate([o_ref[2 * p], o_ref[2 * p + 1]], axis=-1), wuv_ref[p],
                     preferred_element_type=F32).astype(BF16) for p in range(npair)]
    y_ref[...] = h_ref[...] + jnp.dot(jnp.concatenate(parts, axis=-1), wo_ref[...], preferred_element_type=F32)


def _mla_out(h, o, wuv_bd, wo, *, tn):
    n, d = h.shape
    nheads, _, kvl = o.shape
    tok = lambda i: (i, 0)
    return pl.pallas_call(
        _mla_out_kernel,
        grid=(n // tn,),
        in_specs=[pl.BlockSpec((tn, d), tok), pl.BlockSpec((nheads, tn, kvl), lambda i: (0, i, 0)),
                  _full(wuv_bd.shape), _full(wo.shape)],
        out_specs=pl.BlockSpec((tn, d), tok),
        out_shape=jax.ShapeDtypeStruct((n, d), F32),
        compiler_params=_cparams("arbitrary"),
        name="mla_out",
    )(h, o, wuv_bd, wo)


def _rope_tables(pos, half, reps):
    inv = ROPE_THETA ** (-jnp.arange(half, dtype=F32) / half)
    ang = pos.astype(F32)[:, None] * inv[None, :]
    cos, sin = jnp.cos(ang), jnp.sin(ang)
    return (jnp.tile(jnp.concatenate([cos, cos], axis=-1), (1, reps)),
            jnp.tile(jnp.concatenate([-sin, sin], axis=-1), (1, reps)))


def _swap_halves(w):
    half = w.shape[-1] // 2
    return jnp.concatenate([w[..., half:], w[..., :half]], axis=-1)


def _pair_block_diag(w):
    h, k, n = w.shape
    z = jnp.zeros((h // 2, k, n), w.dtype)
    top = jnp.concatenate([w[0::2], z], axis=2)
    bot = jnp.concatenate([z, w[1::2]], axis=2)
    return jnp.concatenate([top, bot], axis=1)


def kernel(x_prompt, x_sample, cache_ckv, cache_krope, page_table, state_h, state_conv, ln_mix, ln_ffn, ln_final, lru_w_in, lru_conv_w, lru_conv_b, lru_gate_a_w, lru_gate_a_b, lru_gate_x_w, lru_gate_x_b, lru_lambda, lru_w_out, kv_norm, w_dkv, ckv_norm, w_kr, w_uk, w_uv, mla_w_dq, mla_q_norm, mla_w_uq, mla_w_o, peer_w_q, peer_subkeys, peer_u, peer_v):
    bp, sp, d = x_prompt.shape
    bd, sd, _ = x_sample.shape
    depth = ln_mix.shape[0]
    assert depth == 2 and lru_w_in.shape[0] == 1 and mla_w_dq.shape[0] == 1
    kvl, nheads, qk_nope = w_uk.shape
    v_head = w_uv.shape[2]
    qk_rope = w_kr.shape[1]
    page = cache_ckv.shape[1]
    past_len = page_table.shape[1] * page
    scale = float(qk_nope + qk_rope) ** -0.5
    np_tok, nd_tok = bp * sp, bd * sd

    lw = (ln_mix[0], lru_w_in[0].astype(BF16), lru_conv_w[0], lru_conv_b[0],
          lru_gate_a_w[0].astype(BF16), lru_gate_a_b[0], lru_gate_x_w[0].astype(BF16), lru_gate_x_b[0],
          lru_lambda[0], lru_w_out[0].astype(BF16))
    hp, hl_p, cb_p = _lru_prompt(x_prompt, *lw)
    hd, hl_d, cb_d = _lru_sample(x_sample, state_conv[:, 0], state_h[:, 0], *lw)
    hp = hp.reshape(np_tok, d)
    hd = hd.reshape(nd_tok, d)

    nkeys = peer_subkeys.shape[3]
    rb = 4 * nkeys
    peer_tabs = [(peer_u[l].astype(BF16),
                  jnp.swapaxes(peer_v[l].astype(BF16).reshape(-1, rb, d), 1, 2)) for l in range(depth)]

    def peer(h, l, final_norm):
        return _peer(h, ln_ffn[l], peer_w_q[l].astype(BF16), peer_subkeys[l].astype(BF16),
                     *peer_tabs[l], ln_final, final_norm=final_norm)

    hp = peer(hp, 0, False)
    hd = peer(hd, 0, False)

    pos_p = jnp.arange(sp, dtype=jnp.int32)
    pos_d = past_len + jnp.arange(sd, dtype=jnp.int32)
    kvw = (kv_norm, w_dkv.astype(BF16), ckv_norm, w_kr.astype(BF16), _swap_halves(w_kr).astype(BF16))
    tn_p, tn_d = 512, nd_tok
    cos_p, sin_p = _rope_tables(pos_p, qk_rope // 2, 1)
    cos_d, sin_d = _rope_tables(jnp.tile(pos_d, bd), qk_rope // 2, 1)
    c_p, kr_p, cb16_p, krb16_p = _shared_kv(hp, *kvw, cos_p, sin_p, tn=tn_p)
    c_d, kr_d, cb16_d, krb16_d = _shared_kv(hd, *kvw, cos_d, sin_d, tn=tn_d)

    wuq = mla_w_uq[0].reshape(-1, nheads, qk_nope + qk_rope)
    w_nope = wuq[:, :, :qk_nope].reshape(-1, nheads * qk_nope).astype(BF16)
    w_rope = wuq[:, :, qk_nope:]
    w_rope_sw = _swap_halves(w_rope).reshape(-1, nheads * qk_rope).astype(BF16)
    w_rope = w_rope.reshape(-1, nheads * qk_rope).astype(BF16)
    wuk_bd = _pair_block_diag(jnp.transpose(w_uk, (1, 2, 0))).astype(BF16)
    wuv_bd = _pair_block_diag(jnp.transpose(w_uv, (1, 0, 2))).astype(BF16)
    qw = (ln_mix[1], mla_w_dq[0].astype(BF16), mla_q_norm[0], w_nope, w_rope, w_rope_sw, wuk_bd)
    ql_p, qr_p = _mla_q(hp, *qw, jnp.tile(cos_p, (1, nheads)), jnp.tile(sin_p, (1, nheads)), scale=scale, tn=tn_p)
    ql_d, qr_d = _mla_q(hd, *qw, jnp.tile(cos_d, (1, nheads)), jnp.tile(sin_d, (1, nheads)), scale=scale, tn=tn_d)

    o_p = _attn_prompt(ql_p, qr_p, cb16_p.reshape(bp, sp, kvl), krb16_p.reshape(bp, sp, qk_rope))

    def per_batch(a):
        return jnp.transpose(a.reshape(nheads, bd, sd, -1), (1, 0, 2, 3)).reshape(bd, nheads * sd, -1)

    pad = ((0, 0), (0, page - sd), (0, 0))
    o_d = _attn_sample(per_batch(ql_d), per_batch(qr_d),
                       jnp.pad(cb16_d.reshape(bd, sd, kvl), pad), jnp.pad(krb16_d.reshape(bd, sd, qk_rope), pad),
                       cache_ckv, jnp.swapaxes(cache_krope, 1, 2), page_table, nheads=nheads, n_new=sd)
    o_d = jnp.transpose(o_d.reshape(bd, nheads, sd, kvl), (1, 0, 2, 3)).reshape(nheads, nd_tok, kvl)
    wo = mla_w_o[0].astype(BF16)
    hp = _mla_out(hp, o_p, wuv_bd, wo, tn=tn_p)
    hd = _mla_out(hd, o_d, wuv_bd, wo, tn=tn_d)

    y_p = peer(hp, 1, True)
    y_d = peer(hd, 1, True)

    return (y_p.reshape(bp, sp, d), y_d.reshape(bd, sd, d),
            hl_p, cb_p[:, None], c_p.reshape(bp, sp, kvl), kr_p.reshape(bp, sp, qk_rope),
            hl_d[:, None], cb_d[:, None], c_d.reshape(bd, sd, kvl), kr_d.reshape(bd, sd, qk_rope))
```

```python
import functools

import jax
import jax.numpy as jnp
from jax import lax
from jax.experimental import pallas as pl
from jax.experimental.pallas import tpu as pltpu

F32 = jnp.float32
BF16 = jnp.bfloat16

RMS_EPS = 1e-6
NEG_INF = -1e30
LRU_C = 8.0
CONV_W = 4
ROPE_THETA = 10000.0
PEER_TOPK = 16

LANES = 128
SUBLANES = 8
VMEM_LIMIT = 56 * 1024 * 1024

_NT = (((1,), (1,)), ((), ()))


def _cparams(*sem):
    return pltpu.CompilerParams(dimension_semantics=sem, vmem_limit_bytes=VMEM_LIMIT)


def _rms(x, g):
    return x * lax.rsqrt(jnp.mean(x * x, axis=-1, keepdims=True) + RMS_EPS) * g


def _gelu(x):
    inner = x * (0.7978845608028654 + (0.7978845608028654 * 0.044715) * (x * x))
    return x * (0.5 + 0.5 * jnp.tanh(inner))


def _sigmoid(x):
    return 1.0 / (1.0 + jnp.exp(-x))


def _softplus(x):
    return jnp.maximum(x, 0.0) + jnp.log(1.0 + jnp.exp(-jnp.abs(x)))


def _full(shape):
    n = len(shape)
    return pl.BlockSpec(shape, lambda *_: (0,) * n)


def _lru_gates(xc, gaw_ref, gab_ref, gxw_ref, gxb_ref, lam_ref):
    nh, bw, _ = gaw_ref.shape
    xcb = xc.astype(BF16)
    r_parts, i_parts = [], []
    for h in range(nh):
        blk = xcb[:, h * bw:(h + 1) * bw]
        r_parts.append(jnp.dot(blk, gaw_ref[h], preferred_element_type=F32))
        i_parts.append(jnp.dot(blk, gxw_ref[h], preferred_element_type=F32))
    r = _sigmoid(jnp.concatenate(r_parts, axis=-1) + gab_ref[...])
    i = _sigmoid(jnp.concatenate(i_parts, axis=-1) + gxb_ref[...])
    log_a = (-LRU_C) * r * _softplus(-lam_ref[...])
    a = jnp.exp(log_a)
    bx = jnp.sqrt(1.0 - jnp.exp(2.0 * log_a)) * i * xc
    return a, bx


def _lru_prompt_kernel(x_ref, g_ref, win_ref, cw_ref, cb_ref, gaw_ref, gab_ref, gxw_ref, gxb_ref,
                       lam_ref, wout_ref, y_ref, hl_ref, cbuf_ref, xcat, a_s, b_s, h_s, hcar):
    t = pl.program_id(1)
    tt, d = x_ref.shape[1], x_ref.shape[2]

    @pl.when(t == 0)
    def _():
        xcat[0:SUBLANES, :] = jnp.zeros((SUBLANES, d), F32)
        hcar[...] = jnp.zeros((1, d), F32)

    x = x_ref[0]
    xn = _rms(x, g_ref[...])
    proj = jnp.dot(xn.astype(BF16), win_ref[...], preferred_element_type=F32)
    gate = _gelu(proj[:, :d])
    xb = proj[:, d:]
    xcat[SUBLANES:SUBLANES + tt, :] = xb
    xc = (cb_ref[...]
          + cw_ref[0:1, :] * xcat[SUBLANES - 3:SUBLANES - 3 + tt, :]
          + cw_ref[1:2, :] * xcat[SUBLANES - 2:SUBLANES - 2 + tt, :]
          + cw_ref[2:3, :] * xcat[SUBLANES - 1:SUBLANES - 1 + tt, :]
          + cw_ref[3:4, :] * xb)
    tail = xcat[tt:tt + SUBLANES, :]
    xcat[0:SUBLANES, :] = tail
    cbuf_ref[0] = tail

    a, bx = _lru_gates(xc, gaw_ref, gab_ref, gxw_ref, gxb_ref, lam_ref)
    a_s[...] = a
    b_s[...] = bx

    def body(i, h):
        h = a_s[pl.ds(i, 1), :] * h + b_s[pl.ds(i, 1), :]
        h_s[pl.ds(i, 1), :] = h
        return h

    h = lax.fori_loop(0, tt, body, hcar[...], unroll=8)
    hcar[...] = h
    hl_ref[0] = h
    y = jnp.dot((h_s[...] * gate).astype(BF16), wout_ref[...], preferred_element_type=F32)
    y_ref[0] = x + y


def _lru_prompt(x, g, win, cw, cb, gaw, gab, gxw, gxb, lam, wout, *, tt=256):
    b, t, d = x.shape
    nh, bw, _ = gaw.shape
    row = lambda v: v.reshape(1, -1)
    wspecs = [_full((1, d)), _full((d, 2 * d)), _full((CONV_W, d)), _full((1, d)),
              _full((nh, bw, bw)), _full((1, d)), _full((nh, bw, bw)), _full((1, d)),
              _full((1, d)), _full((d, d))]
    y, hl, cbuf = pl.pallas_call(
        _lru_prompt_kernel,
        grid=(b, t // tt),
        in_specs=[pl.BlockSpec((1, tt, d), lambda i, j: (i, j, 0))] + wspecs,
        out_specs=[pl.BlockSpec((1, tt, d), lambda i, j: (i, j, 0)),
                   pl.BlockSpec((1, 1, d), lambda i, j: (i, 0, 0)),
                   pl.BlockSpec((1, SUBLANES, d), lambda i, j: (i, 0, 0))],
        out_shape=[jax.ShapeDtypeStruct((b, t, d), F32),
                   jax.ShapeDtypeStruct((b, 1, d), F32),
                   jax.ShapeDtypeStruct((b, SUBLANES, d), F32)],
        scratch_shapes=[pltpu.VMEM((tt + SUBLANES, d), F32), pltpu.VMEM((tt, d), F32),
                        pltpu.VMEM((tt, d), F32), pltpu.VMEM((tt, d), F32), pltpu.VMEM((1, d), F32)],
        compiler_params=_cparams("arbitrary", "arbitrary"),
        name="lru_prompt",
    )(x, row(g), win, cw, row(cb), gaw, row(gab), gxw, row(gxb), row(lam), wout)
    return y, hl, cbuf[:, SUBLANES - (CONV_W - 1):, :]


def _lru_sample_kernel(x_ref, conv_ref, h0_ref, g_ref, win_ref, cw_ref, cb_ref, gaw_ref, gab_ref,
                       gxw_ref, gxb_ref, lam_ref, wout_ref, y_ref, hl_ref, cbuf_ref):
    t, b, d = x_ref.shape
    x = x_ref[...].reshape(t * b, d)
    xn = _rms(x, g_ref[...])
    proj = jnp.dot(xn.astype(BF16), win_ref[...], preferred_element_type=F32)
    gate = _gelu(proj[:, :d])
    xb = proj[:, d:]
    seq = [conv_ref[k] for k in range(CONV_W - 1)] + [xb[k * b:(k + 1) * b, :] for k in range(t)]
    xc_parts = []
    for k in range(t):
        acc = cb_ref[...] + cw_ref[0:1, :] * seq[k]
        for j in range(1, CONV_W):
            acc = acc + cw_ref[j:j + 1, :] * seq[k + j]
        xc_parts.append(acc)
    for k in range(CONV_W - 1):
        cbuf_ref[k] = seq[t + k]
    xc = jnp.concatenate(xc_parts, axis=0)
    a, bx = _lru_gates(xc, gaw_ref, gab_ref, gxw_ref, gxb_ref, lam_ref)
    h = h0_ref[...]
    hs = []
    for k in range(t):
        h = a[k * b:(k + 1) * b, :] * h + bx[k * b:(k + 1) * b, :]
        hs.append(h)
    hl_ref[...] = h
    hall = jnp.concatenate(hs, axis=0)
    y = jnp.dot((hall * gate).astype(BF16), wout_ref[...], preferred_element_type=F32)
    y_ref[...] = (x + y).reshape(t, b, d)


def _lru_sample(x, conv, h0, g, win, cw, cb, gaw, gab, gxw, gxb, lam, wout):
    b, t, d = x.shape
    assert t >= CONV_W - 1
    row = lambda v: v.reshape(1, -1)
    y, hl, cbuf = pl.pallas_call(
        _lru_sample_kernel,
        out_shape=[jax.ShapeDtypeStruct((t, b, d), F32),
                   jax.ShapeDtypeStruct((b, d), F32),
                   jax.ShapeDtypeStruct((CONV_W - 1, b, d), F32)],
        compiler_params=pltpu.CompilerParams(vmem_limit_bytes=VMEM_LIMIT),
        name="lru_sample",
    )(jnp.swapaxes(x, 0, 1), jnp.swapaxes(conv, 0, 1), h0, row(g), win, cw, row(cb),
      gaw, row(gab), gxw, row(gxb), row(lam), wout)
    return jnp.swapaxes(y, 0, 1), hl, jnp.swapaxes(cbuf, 0, 1)


def _cand_pairs(k):
    return [(a, min(k, (k + 1) // (a + 1))) for a in range(k) if (k + 1) // (a + 1) >= 1]


def _bitonic_merge_desc(a):
    n = len(a)
    j = n // 2
    while j >= 1:
        for i in range(n):
            l = i ^ j
            if l > i:
                a[i], a[l] = jnp.maximum(a[i], a[l]), jnp.minimum(a[i], a[l])
        j //= 2
    return a


def _top_desc(vs):
    n = PEER_TOPK
    assert len(vs) <= n and n & (n - 1) == 0
    a = list(vs) + [jnp.full(vs[0].shape, -jnp.inf, F32)] * (n - len(vs))
    k = 2
    while k <= n:
        j = k // 2
        while j >= 1:
            for i in range(n):
                l = i ^ j
                if l > i:
                    hi, lo = jnp.maximum(a[i], a[l]), jnp.minimum(a[i], a[l])
                    a[i], a[l] = (hi, lo) if (i & k) == 0 else (lo, hi)
            j //= 2
        k *= 2
    shift = SUBLANES // 2
    while shift >= 1:
        b = [pltpu.roll(x, shift, 0) for x in a]
        a = _bitonic_merge_desc([jnp.maximum(a[i], b[n - 1 - i]) for i in range(n)])
        shift //= 2
    return a


def _peer_route_kernel(h_ref, g_ref, wq_ref, sk_ref, xnt_ref, rk_ref, eb_ref, cnt_ref, aw_ref,
                       q_s, st_s, sv_s, cand_s):
    tn, d = h_ref.shape
    nh, _, nk, dk2 = sk_ref.shape
    xn = _rms(h_ref[...], g_ref[...])
    xnt = xn.T.astype(BF16)
    tw = xnt_ref.shape[2]
    for k in range(xnt_ref.shape[0]):
        xnt_ref[k] = xnt[:, k * tw:(k + 1) * tw]
    q = jnp.dot(xn.astype(BF16), wq_ref[...], preferred_element_type=F32).astype(BF16)
    for hp in range(2 * nh):
        q_s[hp] = q[:, hp * dk2:(hp + 1) * dk2]

    def scores(hp, carry):
        s = lax.dot_general(sk_ref[hp // 2, hp % 2], q_s[hp], _NT, preferred_element_type=F32)
        st_s[hp] = s
        for ts in range(tn // LANES):
            sl = slice(ts * LANES, (ts + 1) * LANES)
            top = _top_desc([s[k * SUBLANES:(k + 1) * SUBLANES, sl] for k in range(nk // SUBLANES)])
            for r in range(PEER_TOPK):
                sv_s[hp, r:r + 1, sl] = top[r][0:1, :]
        return carry

    lax.fori_loop(0, 2 * nh, scores, 0)

    pairs = _cand_pairs(PEER_TOPK)
    ncand = sum(nb for _, nb in pairs)
    cand_s[...] = jnp.full(cand_s.shape, -jnp.inf, F32)

    def thresholds(h, carry):
        for ts in range(tn // LANES):
            sl = slice(ts * LANES, (ts + 1) * LANES)
            sv0 = sv_s[2 * h, :, sl]
            sv1 = sv_s[2 * h + 1, :, sl]
            off = 0
            for a, nb in pairs:
                cand_s[off:off + nb, :] = sv0[a:a + 1, :] + sv1[0:nb, :]
                off += nb
            cand0 = cand_s[...]
            cur = cand0
            t_hi = None
            for r in range(PEER_TOPK + 1):
                m = jnp.max(cur, axis=0, keepdims=True)
                if r == PEER_TOPK - 1:
                    t_hi = m
                cur = jnp.where(cur == m, -jnp.inf, cur)
            t_mid = 0.5 * (t_hi + m)
            m0 = sv0[0:1, :]
            m1 = sv1[0:1, :]
            z = jnp.sum(jnp.where(cand0 >= t_mid, jnp.exp(cand0 - (m0 + m1)), 0.0), axis=0, keepdims=True)
            s0 = st_s[2 * h, :, sl]
            s1 = st_s[2 * h + 1, :, sl]
            t0 = t_mid - s0
            rank = jnp.zeros(s1.shape, F32)
            count = jnp.zeros(s0.shape, F32)
            for r in range(PEER_TOPK):
                v = sv1[r:r + 1, :]
                rank = jnp.where(s1 < v, r + 1.0, rank)
                count = jnp.where(v >= t0, r + 1.0, count)
            rk_ref[h, ts] = rank.astype(BF16)
            eb_ref[h, ts] = jnp.exp(s1 - m1).astype(BF16)
            cnt_ref[h, ts] = _dup_bf16(count)
            aw_ref[h, ts] = _dup_bf16(jnp.exp(s0 - m0) * (1.0 / z))
        return carry

    assert ncand <= cand_s.shape[0]
    lax.fori_loop(0, nh, thresholds, 0)


def _row_bcast(ref, h, ts, i):
    return ref[h, ts, pl.ds(i, SUBLANES, stride=0), :]


def _dup_bf16(x):
    u = lax.bitcast_convert_type(x.astype(BF16).astype(F32), jnp.uint32)
    return u | (u >> 16)


def _row_bcast_bf16(ref, h, ts, i):
    return pltpu.bitcast(_row_bcast(ref, h, ts, i), BF16)


def _peer_dense_kernel(h_ref, xnt_ref, u_ref, vt_ref, rk_ref, eb_ref, cnt_ref, aw_ref, gf_ref, o_ref,
                       act_a, act_b, p_a, p_b, acc_s, rk_s, eb_s, *, final_norm):
    c = pl.program_id(1)
    nh, _, ic, _ = cnt_ref.shape
    nk = rk_ref.shape[2]
    ntw, _, tw = xnt_ref.shape
    nrg, _, rb = vt_ref.shape
    rg = rb // nk
    jb = min(64, nk)
    n_it = ntw * nrg

    @pl.when(c == 0)
    def _():
        acc_s[...] = jnp.zeros(acc_s.shape, F32)
        rk_s[...] = rk_ref[...]
        eb_s[...] = eb_ref[...]

    def act_dot(it):
        rows = pl.ds((it % nrg) * rb, rb)
        return jnp.dot(u_ref[rows, :], xnt_ref[it // nrg], preferred_element_type=F32)

    def out_dot(it, p_ref):
        acc_s[it // nrg] += jnp.dot(vt_ref[it % nrg], p_ref[...], preferred_element_type=F32)

    def gates(it, act_ref, p_ref, half):
        i8 = pl.ds(((it % nrg) // 2) * SUBLANES, SUBLANES)
        reps = jb // (2 * SUBLANES)

        def row_tile(tile8, row):
            x = jnp.broadcast_to(tile8[row:row + 1, :], (SUBLANES, LANES))
            return jnp.concatenate([pltpu.bitcast(x, BF16)] * reps, axis=0)

        for tsub in range(tw // LANES):
            ts = (it // nrg) * (tw // LANES) + tsub
            sl = slice(tsub * LANES, (tsub + 1) * LANES)
            cnt8 = [cnt_ref[h, ts, i8, :] for h in range(nh)]
            aw8 = [aw_ref[h, ts, i8, :] for h in range(nh)]
            for j0 in range(0, nk, jb):
                for r in range(rg):
                    g = jnp.zeros((jb, LANES), BF16)
                    for h in range(nh):
                        rk = rk_s[h, ts, j0:j0 + jb, :]
                        eb = eb_s[h, ts, j0:j0 + jb, :]
                        cnt = row_tile(cnt8[h], half * rg + r)
                        aw = row_tile(aw8[h], half * rg + r)
                        g = g + jnp.where(rk < cnt, eb, jnp.zeros_like(eb)) * aw
                    es = slice(r * nk + j0, r * nk + j0 + jb)
                    p_ref[es, sl] = _gelu(act_ref[es, sl]).astype(BF16) * g

    act_a[...] = act_dot(0)

    assert nrg % 2 == 0 and 2 * rg == SUBLANES
    for it in range(0, n_it, 2):
        act_b[...] = act_dot(it + 1)
        gates(it, act_a, p_a, 0)
        if it > 0:
            out_dot(it - 1, p_b)
        if it + 2 < n_it:
            act_a[...] = act_dot(it + 2)
        gates(it + 1, act_b, p_b, 1)
        out_dot(it, p_a)
    out_dot(n_it - 1, p_b)

    @pl.when(c == pl.num_programs(1) - 1)
    def _():
        out = h_ref[...] + jnp.concatenate([acc_s[k].T for k in range(ntw)], axis=0)
        if final_norm:
            out = _rms(out, gf_ref[...])
        o_ref[...] = out


def _peer(h, g, wq, sk, u, vt, gf, *, final_norm, tn=512, td=1024, tw=256, ec=1024):
    n, d = h.shape
    nh, _, nk, dk2 = sk.shape
    ne = u.shape[0]
    rb = vt.shape[2]
    ec = min(ec, ne)
    assert ne == nk * nk and n % tn == 0 and ne % ec == 0 and ec % (nk * SUBLANES) == 0
    assert tn % tw == 0 and tw % LANES == 0 and rb % nk == 0 and ec % rb == 0
    ic = ec // nk
    ncand_pad = 56
    nts = tn // LANES
    rt_spec = pl.BlockSpec((nh, nts, nk, LANES), lambda i: (0, i, 0, 0))
    rt_shape = lambda dt: jax.ShapeDtypeStruct((nh, n // LANES, nk, LANES), dt)
    xnt, rk, eb, cnt, aw = pl.pallas_call(
        _peer_route_kernel,
        grid=(n // tn,),
        in_specs=[pl.BlockSpec((tn, d), lambda i: (i, 0)), _full((1, d)), _full(wq.shape), _full(sk.shape)],
        out_specs=[pl.BlockSpec((tn // tw, d, tw), lambda i: (i, 0, 0))] + [rt_spec] * 4,
        out_shape=[jax.ShapeDtypeStruct((n // tw, d, tw), BF16),
                   rt_shape(BF16), rt_shape(BF16), rt_shape(jnp.uint32), rt_shape(jnp.uint32)],
        scratch_shapes=[pltpu.VMEM((2 * nh, tn, dk2), BF16), pltpu.VMEM((2 * nh, nk, tn), F32),
                        pltpu.VMEM((2 * nh, PEER_TOPK, tn), F32), pltpu.VMEM((ncand_pad, LANES), F32)],
        compiler_params=_cparams("arbitrary"),
        name="peer_route",
    )(h, g.reshape(1, d), wq, sk)
    td = min(td, n)
    assert n % td == 0 and td % tw == 0
    nts = td // LANES
    return pl.pallas_call(
        functools.partial(_peer_dense_kernel, final_norm=final_norm),
        grid=(n // td, ne // ec),
        in_specs=[pl.BlockSpec((td, d), lambda i, c: (i, 0)),
                  pl.BlockSpec((td // tw, d, tw), lambda i, c: (i, 0, 0)),
                  pl.BlockSpec((ec, d), lambda i, c: (c, 0)),
                  pl.BlockSpec((ec // rb, d, rb), lambda i, c: (c, 0, 0)),
                  pl.BlockSpec((nh, nts, nk, LANES), lambda i, c: (0, i, 0, 0)),
                  pl.BlockSpec((nh, nts, nk, LANES), lambda i, c: (0, i, 0, 0)),
                  pl.BlockSpec((nh, nts, ic, LANES), lambda i, c: (0, i, c, 0)),
                  pl.BlockSpec((nh, nts, ic, LANES), lambda i, c: (0, i, c, 0)),
                  _full((1, d))],
        out_specs=pl.BlockSpec((td, d), lambda i, c: (i, 0)),
        out_shape=jax.ShapeDtypeStruct((n, d), F32),
        scratch_shapes=[pltpu.VMEM((rb, tw), F32)] * 2 + [pltpu.VMEM((rb, tw), BF16)] * 2
                       + [pltpu.VMEM((td // tw, d, tw), F32)] + [pltpu.VMEM((nh, nts, nk, LANES), BF16)] * 2,
        compiler_params=_cparams("arbitrary", "arbitrary"),
        name="peer_dense",
    )(h, xnt, u, vt, rk, eb, cnt, aw, gf.reshape(1, d))


def _shared_kv_kernel(h_ref, g_ref, wdkv_ref, gc_ref, wkr_ref, wkrs_ref, cos_ref, sin_ref,
                      c_ref, kr_ref, cb_ref, krb_ref):
    hn = _rms(h_ref[...], g_ref[...]).astype(BF16)
    c = _rms(jnp.dot(hn, wdkv_ref[...], preferred_element_type=F32), gc_ref[...])
    kr = (jnp.dot(hn, wkr_ref[...], preferred_element_type=F32) * cos_ref[...]
          + jnp.dot(hn, wkrs_ref[...], preferred_element_type=F32) * sin_ref[...])
    c_ref[...] = c
    kr_ref[...] = kr
    cb_ref[...] = c.astype(BF16)
    krb_ref[...] = kr.astype(BF16)


def _shared_kv(h, g, wdkv, gc, wkr, wkrs, cos, sin, *, tn):
    n, d = h.shape
    kvl, rr = wdkv.shape[1], wkr.shape[1]
    period = cos.shape[0] // tn
    tok = lambda i: (i, 0)
    return pl.pallas_call(
        _shared_kv_kernel,
        grid=(n // tn,),
        in_specs=[pl.BlockSpec((tn, d), tok), _full((1, d)), _full(wdkv.shape), _full((1, kvl)),
                  _full(wkr.shape), _full(wkrs.shape),
                  pl.BlockSpec((tn, rr), lambda i: (i % period, 0)),
                  pl.BlockSpec((tn, rr), lambda i: (i % period, 0))],
        out_specs=[pl.BlockSpec((tn, kvl), tok), pl.BlockSpec((tn, rr), tok),
                   pl.BlockSpec((tn, kvl), tok), pl.BlockSpec((tn, rr), tok)],
        out_shape=[jax.ShapeDtypeStruct((n, kvl), F32), jax.ShapeDtypeStruct((n, rr), F32),
                   jax.ShapeDtypeStruct((n, kvl), BF16), jax.ShapeDtypeStruct((n, rr), BF16)],
        compiler_params=_cparams("arbitrary"),
        name="shared_kv",
    )(h, g.reshape(1, d), wdkv, gc.reshape(1, kvl), wkr, wkrs, cos, sin)


def _mla_q_kernel(h_ref, g_ref, wdq_ref, gq_ref, wn_ref, wr_ref, wrs_ref, wuk_ref, cos_ref, sin_ref,
                  ql_ref, qr_ref, *, scale):
    xn = _rms(h_ref[...], g_ref[...]).astype(BF16)
    cq = _rms(jnp.dot(xn, wdq_ref[...], preferred_element_type=F32), gq_ref[...]).astype(BF16)
    qn = jnp.dot(cq, wn_ref[...], preferred_element_type=F32).astype(BF16)
    qr = (jnp.dot(cq, wr_ref[...], preferred_element_type=F32) * cos_ref[...]
          + jnp.dot(cq, wrs_ref[...], preferred_element_type=F32) * sin_ref[...])
    qr = (qr * scale).astype(BF16)
    nheads, _, rr = qr_ref.shape
    for h in range(nheads):
        qr_ref[h] = qr[:, h * rr:(h + 1) * rr]
    npair, pw, ow = wuk_ref.shape
    for p in range(npair):
        ql = jnp.dot(qn[:, p * pw:(p + 1) * pw], wuk_ref[p], preferred_element_type=F32)
        ql = (ql * scale).astype(BF16)
        ql_ref[2 * p] = ql[:, :ow // 2]
        ql_ref[2 * p + 1] = ql[:, ow // 2:]


def _mla_q(h, g, wdq, gq, wn, wr, wrs, wuk_bd, cos, sin, *, scale, tn):
    n, d = h.shape
    nheads = 2 * wuk_bd.shape[0]
    kvl = wuk_bd.shape[2] // 2
    qr_w = wr.shape[1]
    rr = qr_w // nheads
    period = cos.shape[0] // tn
    tok = lambda i: (i, 0)
    htok = lambda i: (0, i, 0)
    return pl.pallas_call(
        functools.partial(_mla_q_kernel, scale=scale),
        grid=(n // tn,),
        in_specs=[pl.BlockSpec((tn, d), tok), _full((1, d)), _full(wdq.shape), _full((1, wdq.shape[1])),
                  _full(wn.shape), _full(wr.shape), _full(wrs.shape), _full(wuk_bd.shape),
                  pl.BlockSpec((tn, qr_w), lambda i: (i % period, 0)),
                  pl.BlockSpec((tn, qr_w), lambda i: (i % period, 0))],
        out_specs=[pl.BlockSpec((nheads, tn, kvl), htok), pl.BlockSpec((nheads, tn, rr), htok)],
        out_shape=[jax.ShapeDtypeStruct((nheads, n, kvl), BF16), jax.ShapeDtypeStruct((nheads, n, rr), BF16)],
        compiler_params=_cparams("arbitrary"),
        name="mla_q",
    )(h, g.reshape(1, d), wdq, gq.reshape(1, -1), wn, wr, wrs, wuk_bd, cos, sin)


def _softmax_step(s, c, m_s, l_s, acc_s, rows=slice(None)):
    m_prev = m_s[rows]
    m_new = jnp.maximum(m_prev, jnp.max(s, axis=1, keepdims=True))
    alpha = jnp.exp(m_prev - m_new)
    p = jnp.exp(s - m_new)
    l_s[rows] = alpha * l_s[rows] + jnp.sum(p, axis=1, keepdims=True)
    acc_s[rows] = alpha * acc_s[rows] + jnp.dot(p.astype(BF16), c, preferred_element_type=F32)
    m_s[rows] = m_new


def _attn_prompt_kernel(ql_ref, qr_ref, c_ref, kr_ref, o_ref, m_s, l_s, acc_s, *, tq, tk, hc):
    qi, ki = pl.program_id(1), pl.program_id(2)
    nheads, _, kvl = ql_ref.shape
    rr = qr_ref.shape[2]
    q_lo, k_lo = qi * tq, ki * tk

    @pl.when(ki == 0)
    def _():
        m_s[...] = jnp.full(m_s.shape, NEG_INF, F32)
        l_s[...] = jnp.zeros(l_s.shape, F32)
        acc_s[...] = jnp.zeros(acc_s.shape, F32)

    def step(masked):
        c, kr = c_ref[0], kr_ref[0]
        rows = hc * tq
        for h0 in range(0, nheads, hc):
            ql = ql_ref[h0:h0 + hc].reshape(rows, kvl)
            qr = qr_ref[h0:h0 + hc].reshape(rows, rr)
            s = (lax.dot_general(ql, c, _NT, preferred_element_type=F32)
                 + lax.dot_general(qr, kr, _NT, preferred_element_type=F32))
            if masked:
                qpos = q_lo + (lax.broadcasted_iota(jnp.int32, (rows, tk), 0) & (tq - 1))
                kpos = k_lo + lax.broadcasted_iota(jnp.int32, (rows, tk), 1)
                s = jnp.where(kpos <= qpos, s, NEG_INF)
            _softmax_step(s, c, m_s, l_s, acc_s, slice(h0 * tq, (h0 + hc) * tq))

    @pl.when(k_lo + tk - 1 <= q_lo)
    def _():
        step(False)

    @pl.when((k_lo + tk - 1 > q_lo) & (k_lo <= q_lo + tq - 1))
    def _():
        step(True)

    @pl.when(ki == pl.num_programs(2) - 1)
    def _():
        o_ref[...] = (acc_s[...] / l_s[...]).astype(BF16).reshape(nheads, tq, kvl)


def _attn_prompt(ql, qr, cb, krb, *, tq=256, tk=512, hc=2):
    nheads, n, kvl = ql.shape
    rr = qr.shape[2]
    b, t, _ = cb.shape
    assert tq & (tq - 1) == 0 and t % tq == 0 and t % tk == 0 and nheads % hc == 0
    nq = t // tq
    rows = tq * nheads
    qmap = lambda i, q, k: (0, i * nq + q, 0)
    kmap = lambda i, q, k: (i, jnp.minimum(k, (q * tq + tq - 1) // tk), 0)
    return pl.pallas_call(
        functools.partial(_attn_prompt_kernel, tq=tq, tk=tk, hc=hc),
        grid=(b, nq, t // tk),
        in_specs=[pl.BlockSpec((nheads, tq, kvl), qmap), pl.BlockSpec((nheads, tq, rr), qmap),
                  pl.BlockSpec((1, tk, kvl), kmap), pl.BlockSpec((1, tk, rr), kmap)],
        out_specs=pl.BlockSpec((nheads, tq, kvl), qmap),
        out_shape=jax.ShapeDtypeStruct((nheads, n, kvl), BF16),
        scratch_shapes=[pltpu.VMEM((rows, 1), F32), pltpu.VMEM((rows, 1), F32), pltpu.VMEM((rows, kvl), F32)],
        compiler_params=_cparams("arbitrary", "arbitrary", "arbitrary"),
        name="attn_prompt",
    )(ql, qr, cb, krb)


def _attn_sample_kernel(pt_ref, ql_ref, qr_ref, cn_ref, krn_ref, *rest, npp, nheads, n_new):
    del pt_ref
    cpages, krpages = rest[:npp], rest[npp:2 * npp]
    o_ref, m_s, l_s, acc_s = rest[2 * npp:]
    g = pl.program_id(1)
    ql, qr = ql_ref[0], qr_ref[0]
    rows = ql.shape[0]

    @pl.when(g == 0)
    def _():
        m_s[...] = jnp.full(m_s.shape, NEG_INF, F32)
        l_s[...] = jnp.zeros(l_s.shape, F32)
        acc_s[...] = jnp.zeros(acc_s.shape, F32)
        c = cn_ref[0]
        s = (lax.dot_general(ql, c, _NT, preferred_element_type=F32)
             + lax.dot_general(qr, krn_ref[0], _NT, preferred_element_type=F32))
        qpos = lax.rem(lax.broadcasted_iota(jnp.int32, s.shape, 0), n_new)
        kpos = lax.broadcasted_iota(jnp.int32, s.shape, 1)
        s = jnp.where((kpos <= qpos) & (kpos < n_new), s, NEG_INF)
        _softmax_step(s, c, m_s, l_s, acc_s)

    c = jnp.concatenate([r[0].astype(BF16) for r in cpages], axis=0)
    krt = jnp.concatenate([r[0].astype(BF16) for r in krpages], axis=1)
    s = (lax.dot_general(ql, c, _NT, preferred_element_type=F32)
         + jnp.dot(qr, krt, preferred_element_type=F32))
    _softmax_step(s, c, m_s, l_s, acc_s)

    @pl.when(g == pl.num_programs(1) - 1)
    def _():
        o_ref[0] = (acc_s[...] / l_s[...]).astype(BF16)


def _attn_sample(ql, qr, cn, krn, cache_c, cache_krt, page_table, *, nheads, n_new, npp=32):
    b, rows, kvl = ql.shape
    rr = qr.shape[2]
    page = cache_c.shape[1]
    n_pages = page_table.shape[1]
    assert n_pages % npp == 0 and cn.shape[1] == page
    bmap = lambda i, g, pt: (i, 0, 0)
    pmap = lambda j: (lambda i, g, pt: (pt[i, g * npp + j], 0, 0))
    grid_spec = pltpu.PrefetchScalarGridSpec(
        num_scalar_prefetch=1,
        grid=(b, n_pages // npp),
        in_specs=[pl.BlockSpec((1, rows, kvl), bmap), pl.BlockSpec((1, rows, rr), bmap),
                  pl.BlockSpec((1, page, kvl), bmap), pl.BlockSpec((1, page, rr), bmap)]
                 + [pl.BlockSpec((1, page, kvl), pmap(j)) for j in range(npp)]
                 + [pl.BlockSpec((1, rr, page), pmap(j)) for j in range(npp)],
        out_specs=pl.BlockSpec((1, rows, kvl), bmap),
        scratch_shapes=[pltpu.VMEM((rows, 1), F32), pltpu.VMEM((rows, 1), F32), pltpu.VMEM((rows, kvl), F32)],
    )
    return pl.pallas_call(
        functools.partial(_attn_sample_kernel, npp=npp, nheads=nheads, n_new=n_new),
        grid_spec=grid_spec,
        out_shape=jax.ShapeDtypeStruct((b, rows, kvl), BF16),
        compiler_params=_cparams("arbitrary", "arbitrary"),
        name="attn_sample",
    )(page_table, ql, qr, cn, krn, *([cache_c] * npp), *([cache_krt] * npp))


def _mla_out_kernel(h_ref, o_ref, wuv_ref, wo_ref, y_ref):
    npair = wuv_ref.shape[0]
    parts = [jnp.dot(jnp.concatenate([o_ref[2 * p], o_ref[2 * p + 1]], axis=-1), wuv_ref[p],
                     preferred_element_type=F32).astype(BF16) for p in range(npair)]
    y_ref[...] = h_ref[...] + jnp.dot(jnp.concatenate(parts, axis=-1), wo_ref[...], preferred_element_type=F32)


def _mla_out(h, o, wuv_bd, wo, *, tn):
    n, d = h.shape
    nheads, _, kvl = o.shape
    tok = lambda i: (i, 0)
    return pl.pallas_call(
        _mla_out_kernel,
        grid=(n // tn,),
        in_specs=[pl.BlockSpec((tn, d), tok), pl.BlockSpec((nheads, tn, kvl), lambda i: (0, i, 0)),
                  _full(wuv_bd.shape), _full(wo.shape)],
        out_specs=pl.BlockSpec((tn, d), tok),
        out_shape=jax.ShapeDtypeStruct((n, d), F32),
        compiler_params=_cparams("arbitrary"),
        name="mla_out",
    )(h, o, wuv_bd, wo)


def _rope_tables(pos, half, reps):
    inv = ROPE_THETA ** (-jnp.arange(half, dtype=F32) / half)
    ang = pos.astype(F32)[:, None] * inv[None, :]
    cos, sin = jnp.cos(ang), jnp.sin(ang)
    return (jnp.tile(jnp.concatenate([cos, cos], axis=-1), (1, reps)),
            jnp.tile(jnp.concatenate([-sin, sin], axis=-1), (1, reps)))


def _swap_halves(w):
    half = w.shape[-1] // 2
    return jnp.concatenate([w[..., half:], w[..., :half]], axis=-1)


def _pair_block_diag(w):
    h, k, n = w.shape
    z = jnp.zeros((h // 2, k, n), w.dtype)
    top = jnp.concatenate([w[0::2], z], axis=2)
    bot = jnp.concatenate([z, w[1::2]], axis=2)
    return jnp.concatenate([top, bot], axis=1)


def kernel(x_prompt, x_sample, cache_ckv, cache_krope, page_table, state_h, state_conv, ln_mix, ln_ffn, ln_final, lru_w_in, lru_conv_w, lru_conv_b, lru_gate_a_w, lru_gate_a_b, lru_gate_x_w, lru_gate_x_b, lru_lambda, lru_w_out, kv_norm, w_dkv, ckv_norm, w_kr, w_uk, w_uv, mla_w_dq, mla_q_norm, mla_w_uq, mla_w_o, peer_w_q, peer_subkeys, peer_u, peer_v):
    bp, sp, d = x_prompt.shape
    bd, sd, _ = x_sample.shape
    depth = ln_mix.shape[0]
    assert depth == 2 and lru_w_in.shape[0] == 1 and mla_w_dq.shape[0] == 1
    kvl, nheads, qk_nope = w_uk.shape
    v_head = w_uv.shape[2]
    qk_rope = w_kr.shape[1]
    page = cache_ckv.shape[1]
    past_len = page_table.shape[1] * page
    scale = float(qk_nope + qk_rope) ** -0.5
    np_tok, nd_tok = bp * sp, bd * sd

    lw = (ln_mix[0], lru_w_in[0].astype(BF16), lru_conv_w[0], lru_conv_b[0],
          lru_gate_a_w[0].astype(BF16), lru_gate_a_b[0], lru_gate_x_w[0].astype(BF16), lru_gate_x_b[0],
          lru_lambda[0], lru_w_out[0].astype(BF16))
    hp, hl_p, cb_p = _lru_prompt(x_prompt, *lw)
    hd, hl_d, cb_d = _lru_sample(x_sample, state_conv[:, 0], state_h[:, 0], *lw)
    hp = hp.reshape(np_tok, d)
    hd = hd.reshape(nd_tok, d)

    nkeys = peer_subkeys.shape[3]
    rb = 4 * nkeys
    peer_tabs = [(peer_u[l].astype(BF16),
                  jnp.swapaxes(peer_v[l].astype(BF16).reshape(-1, rb, d), 1, 2)) for l in range(depth)]

    def peer(h, l, final_norm):
        return _peer(h, ln_ffn[l], peer_w_q[l].astype(BF16), peer_subkeys[l].astype(BF16),
                     *peer_tabs[l], ln_final, final_norm=final_norm)

    hp = peer(hp, 0, False)
    hd = peer(hd, 0, False)

    pos_p = jnp.arange(sp, dtype=jnp.int32)
    pos_d = past_len + jnp.arange(sd, dtype=jnp.int32)
    kvw = (kv_norm, w_dkv.astype(BF16), ckv_norm, w_kr.astype(BF16), _swap_halves(w_kr).astype(BF16))
    tn_p, tn_d = 512, nd_tok
    cos_p, sin_p = _rope_tables(pos_p, qk_rope // 2, 1)
    cos_d, sin_d = _rope_tables(jnp.tile(pos_d, bd), qk_rope // 2, 1)
    c_p, kr_p, cb16_p, krb16_p = _shared_kv(hp, *kvw, cos_p, sin_p, tn=tn_p)
    c_d, kr_d, cb16_d, krb16_d = _shared_kv(hd, *kvw, cos_d, sin_d, tn=tn_d)

    wuq = mla_w_uq[0].reshape(-1, nheads, qk_nope + qk_rope)
    w_nope = wuq[:, :, :qk_nope].reshape(-1, nheads * qk_nope).astype(BF16)
    w_rope = wuq[:, :, qk_nope:]
    w_rope_sw = _swap_halves(w_rope).reshape(-1, nheads * qk_rope).astype(BF16)
    w_rope = w_rope.reshape(-1, nheads * qk_rope).astype(BF16)
    wuk_bd = _pair_block_diag(jnp.transpose(w_uk, (1, 2, 0))).astype(BF16)
    wuv_bd = _pair_block_diag(jnp.transpose(w_uv, (1, 0, 2))).astype(BF16)
    qw = (ln_mix[1], mla_w_dq[0].astype(BF16), mla_q_norm[0], w_nope, w_rope, w_rope_sw, wuk_bd)
    ql_p, qr_p = _mla_q(hp, *qw, jnp.tile(cos_p, (1, nheads)), jnp.tile(sin_p, (1, nheads)), scale=scale, tn=tn_p)
    ql_d, qr_d = _mla_q(hd, *qw, jnp.tile(cos_d, (1, nheads)), jnp.tile(sin_d, (1, nheads)), scale=scale, tn=tn_d)

    o_p = _attn_prompt(ql_p, qr_p, cb16_p.reshape(bp, sp, kvl), krb16_p.reshape(bp, sp, qk_rope))

    def per_batch(a):
        return jnp.transpose(a.reshape(nheads, bd, sd, -1), (1, 0, 2, 3)).reshape(bd, nheads * sd, -1)

    pad = ((0, 0), (0, page - sd), (0, 0))
    o_d = _attn_sample(per_batch(ql_d), per_batch(qr_d),
                       jnp.pad(cb16_d.reshape(bd, sd, kvl), pad), jnp.pad(krb16_d.reshape(bd, sd, qk_rope), pad),
                       cache_ckv, jnp.swapaxes(cache_krope, 1, 2), page_table, nheads=nheads, n_new=sd)
    o_d = jnp.transpose(o_d.reshape(bd, nheads, sd, kvl), (1, 0, 2, 3)).reshape(nheads, nd_tok, kvl)
    wo = mla_w_o[0].astype(BF16)
    hp = _mla_out(hp, o_p, wuv_bd, wo, tn=tn_p)
    hd = _mla_out(hd, o_d, wuv_bd, wo, tn=tn_d)

    y_p = peer(hp, 1, True)
    y_d = peer(hd, 1, True)

    return (y_p.reshape(bp, sp, d), y_d.reshape(bd, sd, d),
            hl_p, cb_p[:, None], c_p.reshape(bp, sp, kvl), kr_p.reshape(bp, sp, qk_rope),
            hl_d[:, None], cb_d[:, None], c_d.reshape(bd, sd, kvl), kr_d.reshape(bd, sd, qk_rope))
```

```python
import functools

import jax
import jax.numpy as jnp
from jax import lax
from jax.experimental import pallas as pl
from jax.experimental.pallas import tpu as pltpu

F32 = jnp.float32
BF16 = jnp.bfloat16

RMS_EPS = 1e-6
NEG_INF = -1e30
LRU_C = 8.0
CONV_W = 4
ROPE_THETA = 10000.0
PEER_TOPK = 16

LANES = 128
SUBLANES = 8
VMEM_LIMIT = 56 * 1024 * 1024

_NT = (((1,), (1,)), ((), ()))


def _cparams(*sem):
    return pltpu.CompilerParams(dimension_semantics=sem, vmem_limit_bytes=VMEM_LIMIT)


def _rms(x, g):
    return x * lax.rsqrt(jnp.mean(x * x, axis=-1, keepdims=True) + RMS_EPS) * g


def _gelu(x):
    inner = x * (0.7978845608028654 + (0.7978845608028654 * 0.044715) * (x * x))
    return x * (0.5 + 0.5 * jnp.tanh(inner))


def _sigmoid(x):
    return 1.0 / (1.0 + jnp.exp(-x))


def _softplus(x):
    return jnp.maximum(x, 0.0) + jnp.log(1.0 + jnp.exp(-jnp.abs(x)))


def _full(shape):
    n = len(shape)
    return pl.BlockSpec(shape, lambda *_: (0,) * n)


def _lru_gates(xc, gaw_ref, gab_ref, gxw_ref, gxb_ref, lam_ref):
    nh, bw, _ = gaw_ref.shape
    xcb = xc.astype(BF16)
    r_parts, i_parts = [], []
    for h in range(nh):
        blk = xcb[:, h * bw:(h + 1) * bw]
        r_parts.append(jnp.dot(blk, gaw_ref[h], preferred_element_type=F32))
        i_parts.append(jnp.dot(blk, gxw_ref[h], preferred_element_type=F32))
    r = _sigmoid(jnp.concatenate(r_parts, axis=-1) + gab_ref[...])
    i = _sigmoid(jnp.concatenate(i_parts, axis=-1) + gxb_ref[...])
    log_a = (-LRU_C) * r * _softplus(-lam_ref[...])
    a = jnp.exp(log_a)
    bx = jnp.sqrt(1.0 - jnp.exp(2.0 * log_a)) * i * xc
    return a, bx


def _lru_prompt_kernel(x_ref, g_ref, win_ref, cw_ref, cb_ref, gaw_ref, gab_ref, gxw_ref, gxb_ref,
                       lam_ref, wout_ref, y_ref, hl_ref, cbuf_ref, xcat, a_s, b_s, h_s, hcar):
    t = pl.program_id(1)
    tt, d = x_ref.shape[1], x_ref.shape[2]

    @pl.when(t == 0)
    def _():
        xcat[0:SUBLANES, :] = jnp.zeros((SUBLANES, d), F32)
        hcar[...] = jnp.zeros((1, d), F32)

    x = x_ref[0]
    xn = _rms(x, g_ref[...])
    proj = jnp.dot(xn.astype(BF16), win_ref[...], preferred_element_type=F32)
    gate = _gelu(proj[:, :d])
    xb = proj[:, d:]
    xcat[SUBLANES:SUBLANES + tt, :] = xb
    xc = (cb_ref[...]
          + cw_ref[0:1, :] * xcat[SUBLANES - 3:SUBLANES - 3 + tt, :]
          + cw_ref[1:2, :] * xcat[SUBLANES - 2:SUBLANES - 2 + tt, :]
          + cw_ref[2:3, :] * xcat[SUBLANES - 1:SUBLANES - 1 + tt, :]
          + cw_ref[3:4, :] * xb)
    tail = xcat[tt:tt + SUBLANES, :]
    xcat[0:SUBLANES, :] = tail
    cbuf_ref[0] = tail

    a, bx = _lru_gates(xc, gaw_ref, gab_ref, gxw_ref, gxb_ref, lam_ref)
    a_s[...] = a
    b_s[...] = bx

    def body(i, h):
        h = a_s[pl.ds(i, 1), :] * h + b_s[pl.ds(i, 1), :]
        h_s[pl.ds(i, 1), :] = h
        return h

    h = lax.fori_loop(0, tt, body, hcar[...], unroll=8)
    hcar[...] = h
    hl_ref[0] = h
    y = jnp.dot((h_s[...] * gate).astype(BF16), wout_ref[...], preferred_element_type=F32)
    y_ref[0] = x + y


def _lru_prompt(x, g, win, cw, cb, gaw, gab, gxw, gxb, lam, wout, *, tt=512):
    b, t, d = x.shape
    nh, bw, _ = gaw.shape
    row = lambda v: v.reshape(1, -1)
    wspecs = [_full((1, d)), _full((d, 2 * d)), _full((CONV_W, d)), _full((1, d)),
              _full((nh, bw, bw)), _full((1, d)), _full((nh, bw, bw)), _full((1, d)),
              _full((1, d)), _full((d, d))]
    y, hl, cbuf = pl.pallas_call(
        _lru_prompt_kernel,
        grid=(b, t // tt),
        in_specs=[pl.BlockSpec((1, tt, d), lambda i, j: (i, j, 0))] + wspecs,
        out_specs=[pl.BlockSpec((1, tt, d), lambda i, j: (i, j, 0)),
                   pl.BlockSpec((1, 1, d), lambda i, j: (i, 0, 0)),
                   pl.BlockSpec((1, SUBLANES, d), lambda i, j: (i, 0, 0))],
        out_shape=[jax.ShapeDtypeStruct((b, t, d), F32),
                   jax.ShapeDtypeStruct((b, 1, d), F32),
                   jax.ShapeDtypeStruct((b, SUBLANES, d), F32)],
        scratch_shapes=[pltpu.VMEM((tt + SUBLANES, d), F32), pltpu.VMEM((tt, d), F32),
                        pltpu.VMEM((tt, d), F32), pltpu.VMEM((tt, d), F32), pltpu.VMEM((1, d), F32)],
        compiler_params=_cparams("arbitrary", "arbitrary"),
        name="lru_prompt",
    )(x, row(g), win, cw, row(cb), gaw, row(gab), gxw, row(gxb), row(lam), wout)
    return y, hl, cbuf[:, SUBLANES - (CONV_W - 1):, :]


def _lru_sample_kernel(x_ref, conv_ref, h0_ref, g_ref, win_ref, cw_ref, cb_ref, gaw_ref, gab_ref,
                       gxw_ref, gxb_ref, lam_ref, wout_ref, y_ref, hl_ref, cbuf_ref):
    t, b, d = x_ref.shape
    x = x_ref[...].reshape(t * b, d)
    xn = _rms(x, g_ref[...])
    proj = jnp.dot(xn.astype(BF16), win_ref[...], preferred_element_type=F32)
    gate = _gelu(proj[:, :d])
    xb = proj[:, d:]
    seq = [conv_ref[k] for k in range(CONV_W - 1)] + [xb[k * b:(k + 1) * b, :] for k in range(t)]
    xc_parts = []
    for k in range(t):
        acc = cb_ref[...] + cw_ref[0:1, :] * seq[k]
        for j in range(1, CONV_W):
            acc = acc + cw_ref[j:j + 1, :] * seq[k + j]
        xc_parts.append(acc)
    for k in range(CONV_W - 1):
        cbuf_ref[k] = seq[t + k]
    xc = jnp.concatenate(xc_parts, axis=0)
    a, bx = _lru_gates(xc, gaw_ref, gab_ref, gxw_ref, gxb_ref, lam_ref)
    h = h0_ref[...]
    hs = []
    for k in range(t):
        h = a[k * b:(k + 1) * b, :] * h + bx[k * b:(k + 1) * b, :]
        hs.append(h)
    hl_ref[...] = h
    hall = jnp.concatenate(hs, axis=0)
    y = jnp.dot((hall * gate).astype(BF16), wout_ref[...], preferred_element_type=F32)
    y_ref[...] = (x + y).reshape(t, b, d)


def _lru_sample(x, conv, h0, g, win, cw, cb, gaw, gab, gxw, gxb, lam, wout):
    b, t, d = x.shape
    assert t >= CONV_W - 1
    row = lambda v: v.reshape(1, -1)
    y, hl, cbuf = pl.pallas_call(
        _lru_sample_kernel,
        out_shape=[jax.ShapeDtypeStruct((t, b, d), F32),
                   jax.ShapeDtypeStruct((b, d), F32),
                   jax.ShapeDtypeStruct((CONV_W - 1, b, d), F32)],
        compiler_params=pltpu.CompilerParams(vmem_limit_bytes=VMEM_LIMIT),
        name="lru_sample",
    )(jnp.swapaxes(x, 0, 1), jnp.swapaxes(conv, 0, 1), h0, row(g), win, cw, row(cb),
      gaw, row(gab), gxw, row(gxb), row(lam), wout)
    return jnp.swapaxes(y, 0, 1), hl, jnp.swapaxes(cbuf, 0, 1)


def _cand_pairs(k):
    return [(a, min(k, (k + 1) // (a + 1))) for a in range(k) if (k + 1) // (a + 1) >= 1]


def _bitonic_merge_desc(a):
    n = len(a)
    j = n // 2
    while j >= 1:
        for i in range(n):
            l = i ^ j
            if l > i:
                a[i], a[l] = jnp.maximum(a[i], a[l]), jnp.minimum(a[i], a[l])
        j //= 2
    return a


def _top_desc(vs):
    n = PEER_TOPK
    assert len(vs) <= n and n & (n - 1) == 0
    a = list(vs) + [jnp.full(vs[0].shape, -jnp.inf, F32)] * (n - len(vs))
    k = 2
    while k <= n:
        j = k // 2
        while j >= 1:
            for i in range(n):
                l = i ^ j
                if l > i:
                    hi, lo = jnp.maximum(a[i], a[l]), jnp.minimum(a[i], a[l])
                    a[i], a[l] = (hi, lo) if (i & k) == 0 else (lo, hi)
            j //= 2
        k *= 2
    shift = SUBLANES // 2
    while shift >= 1:
        b = [pltpu.roll(x, shift, 0) for x in a]
        a = _bitonic_merge_desc([jnp.maximum(a[i], b[n - 1 - i]) for i in range(n)])
        shift //= 2
    return a


def _peer_route_kernel(h_ref, g_ref, wq_ref, sk_ref, xnt_ref, rk_ref, eb_ref, cnt_ref, aw_ref,
                       q_s, st_s, sv_s, cand_s):
    tn, d = h_ref.shape
    nh, _, nk, dk2 = sk_ref.shape
    xn = _rms(h_ref[...], g_ref[...])
    xnt = xn.T.astype(BF16)
    tw = xnt_ref.shape[2]
    for k in range(xnt_ref.shape[0]):
        xnt_ref[k] = xnt[:, k * tw:(k + 1) * tw]
    q = jnp.dot(xn.astype(BF16), wq_ref[...], preferred_element_type=F32).astype(BF16)
    for hp in range(2 * nh):
        q_s[hp] = q[:, hp * dk2:(hp + 1) * dk2]

    def scores(hp, carry):
        s = lax.dot_general(sk_ref[hp // 2, hp % 2], q_s[hp], _NT, preferred_element_type=F32)
        st_s[hp] = s
        for ts in range(tn // LANES):
            sl = slice(ts * LANES, (ts + 1) * LANES)
            top = _top_desc([s[k * SUBLANES:(k + 1) * SUBLANES, sl] for k in range(nk // SUBLANES)])
            for r in range(PEER_TOPK):
                sv_s[hp, r:r + 1, sl] = top[r][0:1, :]
        return carry

    lax.fori_loop(0, 2 * nh, scores, 0)

    pairs = _cand_pairs(PEER_TOPK)
    ncand = sum(nb for _, nb in pairs)
    cand_s[...] = jnp.full(cand_s.shape, -jnp.inf, F32)

    def thresholds(h, carry):
        for ts in range(tn // LANES):
            sl = slice(ts * LANES, (ts + 1) * LANES)
            sv0 = sv_s[2 * h, :, sl]
            sv1 = sv_s[2 * h + 1, :, sl]
            off = 0
            for a, nb in pairs:
                cand_s[off:off + nb, :] = sv0[a:a + 1, :] + sv1[0:nb, :]
                off += nb
            cand0 = cand_s[...]
            cur = cand0
            t_hi = None
            for r in range(PEER_TOPK + 1):
                m = jnp.max(cur, axis=0, keepdims=True)
                if r == PEER_TOPK - 1:
                    t_hi = m
                cur = jnp.where(cur == m, -jnp.inf, cur)
            t_mid = 0.5 * (t_hi + m)
            m0 = sv0[0:1, :]
            m1 = sv1[0:1, :]
            z = jnp.sum(jnp.where(cand0 >= t_mid, jnp.exp(cand0 - (m0 + m1)), 0.0), axis=0, keepdims=True)
            s0 = st_s[2 * h, :, sl]
            s1 = st_s[2 * h + 1, :, sl]
            t0 = t_mid - s0
            rank = jnp.zeros(s1.shape, F32)
            count = jnp.zeros(s0.shape, F32)
            for r in range(PEER_TOPK):
                v = sv1[r:r + 1, :]
                rank = jnp.where(s1 < v, r + 1.0, rank)
                count = jnp.where(v >= t0, r + 1.0, count)
            rk_ref[h, ts] = rank.astype(BF16)
            eb_ref[h, ts] = jnp.exp(s1 - m1).astype(BF16)
            cnt_ref[h, ts] = _dup_bf16(count)
            aw_ref[h, ts] = _dup_bf16(jnp.exp(s0 - m0) * (1.0 / z))
        return carry

    assert ncand <= cand_s.shape[0]
    lax.fori_loop(0, nh, thresholds, 0)


def _dup_bf16(x):
    u = lax.bitcast_convert_type(x.astype(BF16).astype(F32), jnp.uint32)
    return u | (u >> 16)


def _peer_dense_kernel(h_ref, xnt_ref, u_ref, vt_ref, rk_ref, eb_ref, cnt_ref, aw_ref, gf_ref, o_ref,
                       act_a, act_b, p_a, p_b, acc_s, rk_s, eb_s, *, final_norm):
    c = pl.program_id(1)
    nh = cnt_ref.shape[0]
    nk = rk_ref.shape[2]
    ntw, _, tw = xnt_ref.shape
    nrg, _, rb = vt_ref.shape
    rg = rb // nk
    jb = min(64, nk)
    n_it = ntw * nrg

    @pl.when(c == 0)
    def _():
        acc_s[...] = jnp.zeros(acc_s.shape, F32)
        rk_s[...] = rk_ref[...]
        eb_s[...] = eb_ref[...]

    def act_dot(it):
        rows = pl.ds((it % nrg) * rb, rb)
        return jnp.dot(u_ref[rows, :], xnt_ref[it // nrg], preferred_element_type=F32)

    def out_dot(it, p_ref):
        acc_s[it // nrg] += jnp.dot(vt_ref[it % nrg], p_ref[...], preferred_element_type=F32)

    def gates(it, act_ref, p_ref, half):
        i8 = pl.ds(((it % nrg) // 2) * SUBLANES, SUBLANES)
        reps = jb // (2 * SUBLANES)

        def row_tile(tile8, row):
            x = jnp.broadcast_to(tile8[row:row + 1, :], (SUBLANES, LANES))
            return jnp.concatenate([pltpu.bitcast(x, BF16)] * reps, axis=0)

        for tsub in range(tw // LANES):
            ts = (it // nrg) * (tw // LANES) + tsub
            sl = slice(tsub * LANES, (tsub + 1) * LANES)
            cnt8 = [cnt_ref[h, ts, i8, :] for h in range(nh)]
            aw8 = [aw_ref[h, ts, i8, :] for h in range(nh)]
            for j0 in range(0, nk, jb):
                for r in range(rg):
                    g = jnp.zeros((jb, LANES), BF16)
                    for h in range(nh):
                        rk = rk_s[h, ts, j0:j0 + jb, :]
                        eb = eb_s[h, ts, j0:j0 + jb, :]
                        cnt = row_tile(cnt8[h], half * rg + r)
                        aw = row_tile(aw8[h], half * rg + r)
                        g = g + jnp.where(rk < cnt, eb, jnp.zeros_like(eb)) * aw
                    es = slice(r * nk + j0, r * nk + j0 + jb)
                    p_ref[es, sl] = _gelu(act_ref[es, sl]).astype(BF16) * g

    act_a[...] = act_dot(0)

    assert nrg % 2 == 0 and 2 * rg == SUBLANES
    for it in range(0, n_it, 2):
        act_b[...] = act_dot(it + 1)
        gates(it, act_a, p_a, 0)
        if it > 0:
            out_dot(it - 1, p_b)
        if it + 2 < n_it:
            act_a[...] = act_dot(it + 2)
        gates(it + 1, act_b, p_b, 1)
        out_dot(it, p_a)
    out_dot(n_it - 1, p_b)

    @pl.when(c == pl.num_programs(1) - 1)
    def _():
        out = h_ref[...] + jnp.concatenate([acc_s[k].T for k in range(ntw)], axis=0)
        if final_norm:
            out = _rms(out, gf_ref[...])
        o_ref[...] = out


def _peer(h, g, wq, sk, u, vt, gf, *, final_norm, tn=512, td=1024, tw=256, ec=1024):
    n, d = h.shape
    nh, _, nk, dk2 = sk.shape
    ne = u.shape[0]
    rb = vt.shape[2]
    ec = min(ec, ne)
    assert ne == nk * nk and n % tn == 0 and ne % ec == 0 and ec % (nk * SUBLANES) == 0
    assert tn % tw == 0 and tw % LANES == 0 and rb % nk == 0 and ec % rb == 0
    ic = ec // nk
    ncand = sum(nb for _, nb in _cand_pairs(PEER_TOPK))
    ncand_pad = -(-ncand // SUBLANES) * SUBLANES
    nts = tn // LANES
    rt_spec = pl.BlockSpec((nh, nts, nk, LANES), lambda i: (0, i, 0, 0))
    rt_shape = lambda dt: jax.ShapeDtypeStruct((nh, n // LANES, nk, LANES), dt)
    xnt, rk, eb, cnt, aw = pl.pallas_call(
        _peer_route_kernel,
        grid=(n // tn,),
        in_specs=[pl.BlockSpec((tn, d), lambda i: (i, 0)), _full((1, d)), _full(wq.shape), _full(sk.shape)],
        out_specs=[pl.BlockSpec((tn // tw, d, tw), lambda i: (i, 0, 0))] + [rt_spec] * 4,
        out_shape=[jax.ShapeDtypeStruct((n // tw, d, tw), BF16),
                   rt_shape(BF16), rt_shape(BF16), rt_shape(jnp.uint32), rt_shape(jnp.uint32)],
        scratch_shapes=[pltpu.VMEM((2 * nh, tn, dk2), BF16), pltpu.VMEM((2 * nh, nk, tn), F32),
                        pltpu.VMEM((2 * nh, PEER_TOPK, tn), F32), pltpu.VMEM((ncand_pad, LANES), F32)],
        compiler_params=_cparams("arbitrary"),
        name="peer_route",
    )(h, g.reshape(1, d), wq, sk)
    td = min(td, n)
    assert n % td == 0 and td % tw == 0
    nts = td // LANES
    return pl.pallas_call(
        functools.partial(_peer_dense_kernel, final_norm=final_norm),
        grid=(n // td, ne // ec),
        in_specs=[pl.BlockSpec((td, d), lambda i, c: (i, 0)),
                  pl.BlockSpec((td // tw, d, tw), lambda i, c: (i, 0, 0)),
                  pl.BlockSpec((ec, d), lambda i, c: (c, 0)),
                  pl.BlockSpec((ec // rb, d, rb), lambda i, c: (c, 0, 0)),
                  pl.BlockSpec((nh, nts, nk, LANES), lambda i, c: (0, i, 0, 0)),
                  pl.BlockSpec((nh, nts, nk, LANES), lambda i, c: (0, i, 0, 0)),
                  pl.BlockSpec((nh, nts, ic, LANES), lambda i, c: (0, i, c, 0)),
                  pl.BlockSpec((nh, nts, ic, LANES), lambda i, c: (0, i, c, 0)),
                  _full((1, d))],
        out_specs=pl.BlockSpec((td, d), lambda i, c: (i, 0)),
        out_shape=jax.ShapeDtypeStruct((n, d), F32),
        scratch_shapes=[pltpu.VMEM((rb, tw), F32)] * 2 + [pltpu.VMEM((rb, tw), BF16)] * 2
                       + [pltpu.VMEM((td // tw, d, tw), F32)] + [pltpu.VMEM((nh, nts, nk, LANES), BF16)] * 2,
        compiler_params=_cparams("arbitrary", "arbitrary"),
        name="peer_dense",
    )(h, xnt, u, vt, rk, eb, cnt, aw, gf.reshape(1, d))


def _shared_kv_kernel(h_ref, g_ref, wdkv_ref, gc_ref, wkr_ref, wkrs_ref, cos_ref, sin_ref,
                      c_ref, kr_ref, cb_ref, krb_ref):
    hn = _rms(h_ref[...], g_ref[...]).astype(BF16)
    c = _rms(jnp.dot(hn, wdkv_ref[...], preferred_element_type=F32), gc_ref[...])
    kr = (jnp.dot(hn, wkr_ref[...], preferred_element_type=F32) * cos_ref[...]
          + jnp.dot(hn, wkrs_ref[...], preferred_element_type=F32) * sin_ref[...])
    c_ref[...] = c
    kr_ref[...] = kr
    cb_ref[...] = c.astype(BF16)
    krb_ref[...] = kr.astype(BF16)


def _shared_kv(h, g, wdkv, gc, wkr, wkrs, cos, sin, *, tn):
    n, d = h.shape
    kvl, rr = wdkv.shape[1], wkr.shape[1]
    period = cos.shape[0] // tn
    tok = lambda i: (i, 0)
    return pl.pallas_call(
        _shared_kv_kernel,
        grid=(n // tn,),
        in_specs=[pl.BlockSpec((tn, d), tok), _full((1, d)), _full(wdkv.shape), _full((1, kvl)),
                  _full(wkr.shape), _full(wkrs.shape),
                  pl.BlockSpec((tn, rr), lambda i: (i % period, 0)),
                  pl.BlockSpec((tn, rr), lambda i: (i % period, 0))],
        out_specs=[pl.BlockSpec((tn, kvl), tok), pl.BlockSpec((tn, rr), tok),
                   pl.BlockSpec((tn, kvl), tok), pl.BlockSpec((tn, rr), tok)],
        out_shape=[jax.ShapeDtypeStruct((n, kvl), F32), jax.ShapeDtypeStruct((n, rr), F32),
                   jax.ShapeDtypeStruct((n, kvl), BF16), jax.ShapeDtypeStruct((n, rr), BF16)],
        compiler_params=_cparams("arbitrary"),
        name="shared_kv",
    )(h, g.reshape(1, d), wdkv, gc.reshape(1, kvl), wkr, wkrs, cos, sin)


def _mla_q_kernel(h_ref, g_ref, wdq_ref, gq_ref, wn_ref, wr_ref, wrs_ref, wuk_ref, cos_ref, sin_ref,
                  ql_ref, qr_ref, *, scale):
    xn = _rms(h_ref[...], g_ref[...]).astype(BF16)
    cq = _rms(jnp.dot(xn, wdq_ref[...], preferred_element_type=F32), gq_ref[...]).astype(BF16)
    qn = jnp.dot(cq, wn_ref[...], preferred_element_type=F32).astype(BF16)
    qr = (jnp.dot(cq, wr_ref[...], preferred_element_type=F32) * cos_ref[...]
          + jnp.dot(cq, wrs_ref[...], preferred_element_type=F32) * sin_ref[...])
    qr = (qr * scale).astype(BF16)
    nheads, _, rr = qr_ref.shape
    for h in range(nheads):
        qr_ref[h] = qr[:, h * rr:(h + 1) * rr]
    npair, pw, ow = wuk_ref.shape
    for p in range(npair):
        ql = jnp.dot(qn[:, p * pw:(p + 1) * pw], wuk_ref[p], preferred_element_type=F32)
        ql = (ql * scale).astype(BF16)
        ql_ref[2 * p] = ql[:, :ow // 2]
        ql_ref[2 * p + 1] = ql[:, ow // 2:]


def _mla_q(h, g, wdq, gq, wn, wr, wrs, wuk_bd, cos, sin, *, scale, tn):
    n, d = h.shape
    nheads = 2 * wuk_bd.shape[0]
    kvl = wuk_bd.shape[2] // 2
    qr_w = wr.shape[1]
    rr = qr_w // nheads
    period = cos.shape[0] // tn
    tok = lambda i: (i, 0)
    htok = lambda i: (0, i, 0)
    return pl.pallas_call(
        functools.partial(_mla_q_kernel, scale=scale),
        grid=(n // tn,),
        in_specs=[pl.BlockSpec((tn, d), tok), _full((1, d)), _full(wdq.shape), _full((1, wdq.shape[1])),
                  _full(wn.shape), _full(wr.shape), _full(wrs.shape), _full(wuk_bd.shape),
                  pl.BlockSpec((tn, qr_w), lambda i: (i % period, 0)),
                  pl.BlockSpec((tn, qr_w), lambda i: (i % period, 0))],
        out_specs=[pl.BlockSpec((nheads, tn, kvl), htok), pl.BlockSpec((nheads, tn, rr), htok)],
        out_shape=[jax.ShapeDtypeStruct((nheads, n, kvl), BF16), jax.ShapeDtypeStruct((nheads, n, rr), BF16)],
        compiler_params=_cparams("arbitrary"),
        name="mla_q",
    )(h, g.reshape(1, d), wdq, gq.reshape(1, -1), wn, wr, wrs, wuk_bd, cos, sin)


def _softmax_step(s, c, m_s, l_s, acc_s, rows=slice(None)):
    m_prev = m_s[rows]
    m_new = jnp.maximum(m_prev, jnp.max(s, axis=1, keepdims=True))
    alpha = jnp.exp(m_prev - m_new)
    p = jnp.exp(s - m_new)
    l_s[rows] = alpha * l_s[rows] + jnp.sum(p, axis=1, keepdims=True)
    acc_s[rows] = alpha * acc_s[rows] + jnp.dot(p.astype(BF16), c, preferred_element_type=F32)
    m_s[rows] = m_new


def _attn_prompt_kernel(ql_ref, qr_ref, c_ref, kr_ref, o_ref, m_s, l_s, acc_s, *, tq, tk, hc):
    qi, ki = pl.program_id(1), pl.program_id(2)
    nheads, _, kvl = ql_ref.shape
    rr = qr_ref.shape[2]
    q_lo, k_lo = qi * tq, ki * tk

    @pl.when(ki == 0)
    def _():
        m_s[...] = jnp.full(m_s.shape, NEG_INF, F32)
        l_s[...] = jnp.zeros(l_s.shape, F32)
        acc_s[...] = jnp.zeros(acc_s.shape, F32)

    def step(masked):
        c, kr = c_ref[0], kr_ref[0]
        rows = hc * tq
        for h0 in range(0, nheads, hc):
            ql = ql_ref[h0:h0 + hc].reshape(rows, kvl)
            qr = qr_ref[h0:h0 + hc].reshape(rows, rr)
            s = (lax.dot_general(ql, c, _NT, preferred_element_type=F32)
                 + lax.dot_general(qr, kr, _NT, preferred_element_type=F32))
            if masked:
                qpos = q_lo + (lax.broadcasted_iota(jnp.int32, (rows, tk), 0) & (tq - 1))
                kpos = k_lo + lax.broadcasted_iota(jnp.int32, (rows, tk), 1)
                s = jnp.where(kpos <= qpos, s, NEG_INF)
            _softmax_step(s, c, m_s, l_s, acc_s, slice(h0 * tq, (h0 + hc) * tq))

    @pl.when(k_lo + tk - 1 <= q_lo)
    def _():
        step(False)

    @pl.when((k_lo + tk - 1 > q_lo) & (k_lo <= q_lo + tq - 1))
    def _():
        step(True)

    @pl.when(ki == pl.num_programs(2) - 1)
    def _():
        o_ref[...] = (acc_s[...] / l_s[...]).astype(BF16).reshape(nheads, tq, kvl)


def _attn_prompt(ql, qr, cb, krb, *, tq=256, tk=512, hc=2):
    nheads, n, kvl = ql.shape
    rr = qr.shape[2]
    b, t, _ = cb.shape
    assert tq & (tq - 1) == 0 and t % tq == 0 and t % tk == 0 and nheads % hc == 0
    nq = t // tq
    rows = tq * nheads
    qmap = lambda i, q, k: (0, i * nq + q, 0)
    kmap = lambda i, q, k: (i, jnp.minimum(k, (q * tq + tq - 1) // tk), 0)
    return pl.pallas_call(
        functools.partial(_attn_prompt_kernel, tq=tq, tk=tk, hc=hc),
        grid=(b, nq, t // tk),
        in_specs=[pl.BlockSpec((nheads, tq, kvl), qmap), pl.BlockSpec((nheads, tq, rr), qmap),
                  pl.BlockSpec((1, tk, kvl), kmap), pl.BlockSpec((1, tk, rr), kmap)],
        out_specs=pl.BlockSpec((nheads, tq, kvl), qmap),
        out_shape=jax.ShapeDtypeStruct((nheads, n, kvl), BF16),
        scratch_shapes=[pltpu.VMEM((rows, 1), F32), pltpu.VMEM((rows, 1), F32), pltpu.VMEM((rows, kvl), F32)],
        compiler_params=_cparams("arbitrary", "arbitrary", "arbitrary"),
        name="attn_prompt",
    )(ql, qr, cb, krb)


def _attn_sample_kernel(pt_ref, ql_ref, qr_ref, cn_ref, krn_ref, *rest, npp, n_new):
    del pt_ref
    cpages, krpages = rest[:npp], rest[npp:2 * npp]
    o_ref, m_s, l_s, acc_s = rest[2 * npp:]
    g = pl.program_id(1)
    ql, qr = ql_ref[0], qr_ref[0]

    @pl.when(g == 0)
    def _():
        m_s[...] = jnp.full(m_s.shape, NEG_INF, F32)
        l_s[...] = jnp.zeros(l_s.shape, F32)
        acc_s[...] = jnp.zeros(acc_s.shape, F32)
        c = cn_ref[0]
        s = (lax.dot_general(ql, c, _NT, preferred_element_type=F32)
             + lax.dot_general(qr, krn_ref[0], _NT, preferred_element_type=F32))
        qpos = lax.rem(lax.broadcasted_iota(jnp.int32, s.shape, 0), n_new)
        kpos = lax.broadcasted_iota(jnp.int32, s.shape, 1)
        s = jnp.where((kpos <= qpos) & (kpos < n_new), s, NEG_INF)
        _softmax_step(s, c, m_s, l_s, acc_s)

    c = jnp.concatenate([r[0].astype(BF16) for r in cpages], axis=0)
    krt = jnp.concatenate([r[0].astype(BF16) for r in krpages], axis=1)
    s = (lax.dot_general(ql, c, _NT, preferred_element_type=F32)
         + jnp.dot(qr, krt, preferred_element_type=F32))
    _softmax_step(s, c, m_s, l_s, acc_s)

    @pl.when(g == pl.num_programs(1) - 1)
    def _():
        o_ref[0] = (acc_s[...] / l_s[...]).astype(BF16)


def _attn_sample(ql, qr, cn, krn, cache_c, cache_krt, page_table, *, n_new, npp=64):
    b, rows, kvl = ql.shape
    rr = qr.shape[2]
    page = cache_c.shape[1]
    n_pages = page_table.shape[1]
    npp = min(npp, n_pages)
    assert n_pages % npp == 0 and cn.shape[1] == page
    bmap = lambda i, g, pt: (i, 0, 0)
    pmap = lambda j: (lambda i, g, pt: (pt[i, g * npp + j], 0, 0))
    grid_spec = pltpu.PrefetchScalarGridSpec(
        num_scalar_prefetch=1,
        grid=(b, n_pages // npp),
        in_specs=[pl.BlockSpec((1, rows, kvl), bmap), pl.BlockSpec((1, rows, rr), bmap),
                  pl.BlockSpec((1, page, kvl), bmap), pl.BlockSpec((1, page, rr), bmap)]
                 + [pl.BlockSpec((1, page, kvl), pmap(j)) for j in range(npp)]
                 + [pl.BlockSpec((1, rr, page), pmap(j)) for j in range(npp)],
        out_specs=pl.BlockSpec((1, rows, kvl), bmap),
        scratch_shapes=[pltpu.VMEM((rows, 1), F32), pltpu.VMEM((rows, 1), F32), pltpu.VMEM((rows, kvl), F32)],
    )
    return pl.pallas_call(
        functools.partial(_attn_sample_kernel, npp=npp, n_new=n_new),
        grid_spec=grid_spec,
        out_shape=jax.ShapeDtypeStruct((b, rows, kvl), BF16),
        compiler_params=_cparams("arbitrary", "arbitrary"),
        name="attn_sample",
    )(page_table, ql, qr, cn, krn, *([cache_c] * npp), *([cache_krt] * npp))


def _mla_out_kernel(h_ref, o_ref, wuv_ref, wo_ref, y_ref):
    npair = wuv_ref.shape[0]
    parts = [jnp.dot(jnp.concatenate([o_ref[2 * p], o_ref[2 * p + 1]], axis=-1), wuv_ref[p],
                     preferred_element_type=F32).astype(BF16) for p in range(npair)]
    y_ref[...] = h_ref[...] + jnp.dot(jnp.concatenate(parts, axis=-1), wo_ref[...], preferred_element_type=F32)


def _mla_out(h, o, wuv_bd, wo, *, tn):
    n, d = h.shape
    nheads, _, kvl = o.shape
    tok = lambda i: (i, 0)
    return pl.pallas_call(
        _mla_out_kernel,
        grid=(n // tn,),
        in_specs=[pl.BlockSpec((tn, d), tok), pl.BlockSpec((nheads, tn, kvl), lambda i: (0, i, 0)),
                  _full(wuv_bd.shape), _full(wo.shape)],
        out_specs=pl.BlockSpec((tn, d), tok),
        out_shape=jax.ShapeDtypeStruct((n, d), F32),
        compiler_params=_cparams("arbitrary"),
        name="mla_out",
    )(h, o, wuv_bd, wo)


def _rope_tables(pos, half, reps):
    inv = ROPE_THETA ** (-jnp.arange(half, dtype=F32) / half)
    ang = pos.astype(F32)[:, None] * inv[None, :]
    cos, sin = jnp.cos(ang), jnp.sin(ang)
    return (jnp.tile(jnp.concatenate([cos, cos], axis=-1), (1, reps)),
            jnp.tile(jnp.concatenate([-sin, sin], axis=-1), (1, reps)))


def _swap_halves(w):
    half = w.shape[-1] // 2
    return jnp.concatenate([w[..., half:], w[..., :half]], axis=-1)


def _pair_block_diag(w):
    h, k, n = w.shape
    z = jnp.zeros((h // 2, k, n), w.dtype)
    top = jnp.concatenate([w[0::2], z], axis=2)
    bot = jnp.concatenate([z, w[1::2]], axis=2)
    return jnp.concatenate([top, bot], axis=1)


def kernel(x_prompt, x_sample, cache_ckv, cache_krope, page_table, state_h, state_conv, ln_mix, ln_ffn, ln_final, lru_w_in, lru_conv_w, lru_conv_b, lru_gate_a_w, lru_gate_a_b, lru_gate_x_w, lru_gate_x_b, lru_lambda, lru_w_out, kv_norm, w_dkv, ckv_norm, w_kr, w_uk, w_uv, mla_w_dq, mla_q_norm, mla_w_uq, mla_w_o, peer_w_q, peer_subkeys, peer_u, peer_v):
    bp, sp, d = x_prompt.shape
    bd, sd, _ = x_sample.shape
    depth = ln_mix.shape[0]
    assert depth == 2 and lru_w_in.shape[0] == 1 and mla_w_dq.shape[0] == 1
    kvl, nheads, qk_nope = w_uk.shape
    v_head = w_uv.shape[2]
    qk_rope = w_kr.shape[1]
    page = cache_ckv.shape[1]
    past_len = page_table.shape[1] * page
    scale = float(qk_nope + qk_rope) ** -0.5
    np_tok, nd_tok = bp * sp, bd * sd

    lw = (ln_mix[0], lru_w_in[0].astype(BF16), lru_conv_w[0], lru_conv_b[0],
          lru_gate_a_w[0].astype(BF16), lru_gate_a_b[0], lru_gate_x_w[0].astype(BF16), lru_gate_x_b[0],
          lru_lambda[0], lru_w_out[0].astype(BF16))
    hp, hl_p, cb_p = _lru_prompt(x_prompt, *lw)
    hd, hl_d, cb_d = _lru_sample(x_sample, state_conv[:, 0], state_h[:, 0], *lw)
    hp = hp.reshape(np_tok, d)
    hd = hd.reshape(nd_tok, d)

    nkeys = peer_subkeys.shape[3]
    rb = 4 * nkeys
    peer_tabs = [(peer_u[l].astype(BF16),
                  jnp.swapaxes(peer_v[l].astype(BF16).reshape(-1, rb, d), 1, 2)) for l in range(depth)]

    def peer(h, l, final_norm):
        return _peer(h, ln_ffn[l], peer_w_q[l].astype(BF16), peer_subkeys[l].astype(BF16),
                     *peer_tabs[l], ln_final, final_norm=final_norm)

    hp = peer(hp, 0, False)
    hd = peer(hd, 0, False)

    pos_p = jnp.arange(sp, dtype=jnp.int32)
    pos_d = past_len + jnp.arange(sd, dtype=jnp.int32)
    kvw = (kv_norm, w_dkv.astype(BF16), ckv_norm, w_kr.astype(BF16), _swap_halves(w_kr).astype(BF16))
    tn_p, tn_d = 512, nd_tok
    cos_p, sin_p = _rope_tables(pos_p, qk_rope // 2, 1)
    cos_d, sin_d = _rope_tables(jnp.tile(pos_d, bd), qk_rope // 2, 1)
    c_p, kr_p, cb16_p, krb16_p = _shared_kv(hp, *kvw, cos_p, sin_p, tn=tn_p)
    c_d, kr_d, cb16_d, krb16_d = _shared_kv(hd, *kvw, cos_d, sin_d, tn=tn_d)

    wuq = mla_w_uq[0].reshape(-1, nheads, qk_nope + qk_rope)
    w_nope = wuq[:, :, :qk_nope].reshape(-1, nheads * qk_nope).astype(BF16)
    w_rope = wuq[:, :, qk_nope:]
    w_rope_sw = _swap_halves(w_rope).reshape(-1, nheads * qk_rope).astype(BF16)
    w_rope = w_rope.reshape(-1, nheads * qk_rope).astype(BF16)
    wuk_bd = _pair_block_diag(jnp.transpose(w_uk, (1, 2, 0))).astype(BF16)
    wuv_bd = _pair_block_diag(jnp.transpose(w_uv, (1, 0, 2))).astype(BF16)
    qw = (ln_mix[1], mla_w_dq[0].astype(BF16), mla_q_norm[0], w_nope, w_rope, w_rope_sw, wuk_bd)
    ql_p, qr_p = _mla_q(hp, *qw, jnp.tile(cos_p, (1, nheads)), jnp.tile(sin_p, (1, nheads)), scale=scale, tn=tn_p)
    ql_d, qr_d = _mla_q(hd, *qw, jnp.tile(cos_d, (1, nheads)), jnp.tile(sin_d, (1, nheads)), scale=scale, tn=tn_d)

    o_p = _attn_prompt(ql_p, qr_p, cb16_p.reshape(bp, sp, kvl), krb16_p.reshape(bp, sp, qk_rope))

    def per_batch(a):
        return jnp.transpose(a.reshape(nheads, bd, sd, -1), (1, 0, 2, 3)).reshape(bd, nheads * sd, -1)

    pad = ((0, 0), (0, page - sd), (0, 0))
    o_d = _attn_sample(per_batch(ql_d), per_batch(qr_d),
                       jnp.pad(cb16_d.reshape(bd, sd, kvl), pad), jnp.pad(krb16_d.reshape(bd, sd, qk_rope), pad),
                       cache_ckv, jnp.swapaxes(cache_krope, 1, 2), page_table, n_new=sd)
    o_d = jnp.transpose(o_d.reshape(bd, nheads, sd, kvl), (1, 0, 2, 3)).reshape(nheads, nd_tok, kvl)
    wo = mla_w_o[0].astype(BF16)
    hp = _mla_out(hp, o_p, wuv_bd, wo, tn=tn_p)
    hd = _mla_out(hd, o_d, wuv_bd, wo, tn=tn_d)

    y_p = peer(hp, 1, True)
    y_d = peer(hd, 1, True)

    return (y_p.reshape(bp, sp, d), y_d.reshape(bd, sd, d),
            hl_p, cb_p[:, None], c_p.reshape(bp, sp, kvl), kr_p.reshape(bp, sp, qk_rope),
            hl_d[:, None], cb_d[:, None], c_d.reshape(bd, sd, kvl), kr_d.reshape(bd, sd, qk_rope))
```
